```python
import jax, jax.numpy as jnp
from jax import lax
import numpy as np

D_MODEL = 4096
BATCH = 1
SEQ = 16384
DEPTH = 2
DEC_BATCH = 4
DEC_SEQ = 2048
PAST_LEN = 128

GRID_W = 64
N_MIXERS = 2
N_EVEN_LAYERS = (DEPTH + 1) // 2
N_ODD_LAYERS = DEPTH // 2
HEAD_DIM = 128
N_Q_HEADS = D_MODEL // HEAD_DIM
N_KV_HEADS = N_Q_HEADS // 4
Q_BLOCK = 128
ROPE_THETA = 10000.0
ROPE_AXIS_DIM = HEAD_DIM // 2
GLA_HEADS = 4
GLA_KEY_DIM = D_MODEL // 2
GLA_VAL_DIM = D_MODEL
GLA_DK = GLA_KEY_DIM // GLA_HEADS
GLA_DV = GLA_VAL_DIM // GLA_HEADS
GLA_GATE_RANK = 16
GLA_GATE_TAU = 16.0
GLA_CHUNK = 64
D_FF = 2 * D_MODEL
N_EXPERTS = 8
TOP_K = 2
MOE_D_FF = D_MODEL // 4
EPS = 1e-6

kernel_name = "hybrid_axial_gqa_gla_moe_encoder"


def rmsnorm(x, w):
    x32 = x.astype(jnp.float32)
    y = x32 * lax.rsqrt(jnp.mean(x32 * x32, axis=-1, keepdims=True) + EPS)
    return y.astype(x.dtype) * w


def axial_rope_tables(seq_len):
    rows = seq_len // GRID_W
    row = jnp.repeat(jnp.arange(rows), GRID_W).astype(jnp.float32)
    col = jnp.tile(jnp.arange(GRID_W), rows).astype(jnp.float32)
    inv_freq = ROPE_THETA ** (-jnp.arange(0, ROPE_AXIS_DIM, 2, dtype=jnp.float32) / ROPE_AXIS_DIM)
    ang_r = row[:, None] * inv_freq
    ang_c = col[:, None] * inv_freq
    ang = jnp.concatenate([ang_r, ang_r, ang_c, ang_c], axis=-1)
    return jnp.cos(ang), jnp.sin(ang)


def rotate_half_segments(x):
    xr = x.reshape(x.shape[:-1] + (2, 2, ROPE_AXIS_DIM // 2))
    x1 = xr[..., 0, :]
    x2 = xr[..., 1, :]
    return jnp.stack([-x2, x1], axis=-2).reshape(x.shape)


def apply_rope(x, cos, sin):
    return (x * cos + rotate_half_segments(x) * sin).astype(x.dtype)


def axial_gqa_attention(h, wq, wk, wv, q_norm, k_norm, wo):
    b, t, _ = h.shape
    q = (h @ wq).reshape(b, t, N_Q_HEADS, HEAD_DIM)
    k = (h @ wk).reshape(b, t, N_KV_HEADS, HEAD_DIM)
    v = (h @ wv).reshape(b, t, N_KV_HEADS, HEAD_DIM)
    cos, sin = axial_rope_tables(t)
    cos = cos[:, None, :]
    sin = sin[:, None, :]
    q = apply_rope(rmsnorm(q, q_norm), cos, sin)
    k = apply_rope(rmsnorm(k, k_norm), cos, sin)
    n_blk = t // Q_BLOCK
    group = N_Q_HEADS // N_KV_HEADS
    qb = q.reshape(b, n_blk, Q_BLOCK, N_KV_HEADS, group, HEAD_DIM).transpose(1, 0, 3, 4, 2, 5)
    kt = k.transpose(0, 2, 1, 3)
    vt = v.transpose(0, 2, 1, 3)
    scale = HEAD_DIM ** -0.5

    def block(qblk):
        s = jnp.einsum('bkgqd,bksd->bkgqs', qblk, kt).astype(jnp.float32) * scale
        p = jax.nn.softmax(s, axis=-1).astype(vt.dtype)
        return jnp.einsum('bkgqs,bksd->bkgqd', p, vt)

    o = lax.map(block, qb)
    o = o.transpose(1, 0, 4, 2, 3, 5).reshape(b, t, N_Q_HEADS * HEAD_DIM)
    return o @ wo


def gla_chunked(q, k, v, g, strict):
    b, hh, t, dk = q.shape
    dv = v.shape[-1]
    n = t // GLA_CHUNK

    def to_chunks(a):
        return a.reshape(b, hh, n, GLA_CHUNK, a.shape[-1]).transpose(2, 0, 1, 3, 4)

    xs = (to_chunks(q), to_chunks(k), to_chunks(v), to_chunks(g))
    idx = jnp.arange(GLA_CHUNK)
    mask = (idx[:, None] > idx[None, :]) if strict else (idx[:, None] >= idx[None, :])

    def step(state, inp):
        qi, ki, vi, gi = inp
        bcum = jnp.cumsum(gi, axis=-2)
        q_t = qi * jnp.exp(bcum)
        k_t = ki * jnp.exp(-bcum)
        attn = jnp.where(mask, jnp.einsum('bhcd,bhsd->bhcs', q_t, k_t), 0.0)
        o = jnp.einsum('bhcs,bhsv->bhcv', attn, vi) + jnp.einsum('bhcd,bhdv->bhcv', q_t, state)
        b_last = bcum[:, :, -1:, :]
        k_dec = ki * jnp.exp(b_last - bcum)
        state = state * jnp.exp(b_last)[:, :, 0, :, None] + jnp.einsum('bhcd,bhcv->bhdv', k_dec, vi)
        return state, o

    s0 = jnp.zeros((b, hh, dk, dv), q.dtype)
    _, o = lax.scan(step, s0, xs)
    return o.transpose(1, 2, 0, 3, 4).reshape(b, hh, t, dv)


def bidir_gla(h, wq, wk, wv, wg, wa1_f, wa2_f, ba_f, wa1_b, wa2_b, ba_b, o_norm, wo):
    b, t, _ = h.shape

    def heads(a, d):
        return a.reshape(b, t, GLA_HEADS, d).transpose(0, 2, 1, 3).astype(jnp.float32)

    q = heads(h @ wq, GLA_DK) * (GLA_DK ** -0.5)
    k = heads(h @ wk, GLA_DK)
    v = heads(h @ wv, GLA_DV)

    def log_decay(wa1, wa2, ba):
        logits = ((h @ wa1) @ wa2 + ba).astype(jnp.float32)
        return heads(jax.nn.log_sigmoid(logits) / GLA_GATE_TAU, GLA_DK)

    g_f = log_decay(wa1_f, wa2_f, ba_f)
    g_b = log_decay(wa1_b, wa2_b, ba_b)
    o_f = gla_chunked(q, k, v, g_f, strict=False)

    def flip(a):
        return jnp.flip(a, axis=2)

    o_b = flip(gla_chunked(flip(q), flip(k), flip(v), flip(g_b), strict=True))
    o = (o_f + o_b).transpose(0, 2, 1, 3)
    o = rmsnorm(o, o_norm).astype(h.dtype)
    gate = jax.nn.silu(h @ wg).reshape(b, t, GLA_HEADS, GLA_DV)
    return (o * gate).reshape(b, t, GLA_VAL_DIM) @ wo


def swiglu(h, w1, w3, w2):
    return (jax.nn.silu(h @ w1) * (h @ w3)) @ w2


def moe_swiglu(h, router, we1, we3, we2):
    b, t, d = h.shape
    hf = h.reshape(b * t, d)
    logits = (hf @ router).astype(jnp.float32)
    top_val, top_idx = lax.top_k(logits, TOP_K)
    top_w = jax.nn.softmax(top_val, axis=-1)
    gates = jnp.sum(jax.nn.one_hot(top_idx, N_EXPERTS, dtype=jnp.float32) * top_w[..., None], axis=1)
    gates = gates.astype(hf.dtype)
    out = jnp.zeros_like(hf)
    for e in range(N_EXPERTS):
        out = out + gates[:, e:e + 1] * swiglu(hf, we1[e], we3[e], we2[e])
    return out.reshape(b, t, d)


def trunk(x, params):
    (norm_mix, norm_ffn, norm_final,
     attn_wq, attn_wk, attn_wv, attn_q_norm, attn_k_norm, attn_wo,
     gla_wq, gla_wk, gla_wv, gla_wg, gla_wa1_f, gla_wa2_f, gla_ba_f,
     gla_wa1_b, gla_wa2_b, gla_ba_b, gla_o_norm, gla_wo,
     ffn_w1, ffn_w3, ffn_w2,
     moe_router, moe_w1, moe_w3, moe_w2) = params
    for i in range(DEPTH):
        j = i // N_MIXERS
        h = rmsnorm(x, norm_mix[i])
        if i % N_MIXERS == 0:
            x = x + axial_gqa_attention(h, attn_wq[j], attn_wk[j], attn_wv[j],
                                        attn_q_norm[j], attn_k_norm[j], attn_wo[j])
        else:
            x = x + bidir_gla(h, gla_wq[j], gla_wk[j], gla_wv[j], gla_wg[j],
                              gla_wa1_f[j], gla_wa2_f[j], gla_ba_f[j],
                              gla_wa1_b[j], gla_wa2_b[j], gla_ba_b[j], gla_o_norm[j], gla_wo[j])
        h = rmsnorm(x, norm_ffn[i])
        if i % 2 == 0:
            x = x + swiglu(h, ffn_w1[j], ffn_w3[j], ffn_w2[j])
        else:
            x = x + moe_swiglu(h, moe_router[j], moe_w1[j], moe_w3[j], moe_w2[j])
    return rmsnorm(x, norm_final)


def setup_inputs(seed: int = 0) -> dict:
    key = jax.random.key(seed)
    ks = iter(jax.random.split(key, 40))

    def dense(shape, fan_in):
        return jax.random.normal(next(ks), shape, jnp.float32) * (fan_in ** -0.5)

    def gain(shape):
        return 1.0 + 0.02 * jax.random.normal(next(ks), shape, jnp.float32)

    def bias(shape):
        return 0.02 * jax.random.normal(next(ks), shape, jnp.float32)

    ne, no = N_EVEN_LAYERS, N_ODD_LAYERS
    return {
        "x_prompt": jax.random.normal(next(ks), (BATCH, SEQ, D_MODEL), jnp.float32),
        "x_sample": jax.random.normal(next(ks), (DEC_BATCH, DEC_SEQ, D_MODEL), jnp.float32),
        "norm_mix": gain((DEPTH, D_MODEL)),
        "norm_ffn": gain((DEPTH, D_MODEL)),
        "norm_final": gain((D_MODEL,)),
        "attn_wq": dense((ne, D_MODEL, N_Q_HEADS * HEAD_DIM), D_MODEL),
        "attn_wk": dense((ne, D_MODEL, N_KV_HEADS * HEAD_DIM), D_MODEL),
        "attn_wv": dense((ne, D_MODEL, N_KV_HEADS * HEAD_DIM), D_MODEL),
        "attn_q_norm": gain((ne, HEAD_DIM)),
        "attn_k_norm": gain((ne, HEAD_DIM)),
        "attn_wo": dense((ne, N_Q_HEADS * HEAD_DIM, D_MODEL), N_Q_HEADS * HEAD_DIM),
        "gla_wq": dense((no, D_MODEL, GLA_KEY_DIM), D_MODEL),
        "gla_wk": dense((no, D_MODEL, GLA_KEY_DIM), D_MODEL),
        "gla_wv": dense((no, D_MODEL, GLA_VAL_DIM), D_MODEL),
        "gla_wg": dense((no, D_MODEL, GLA_VAL_DIM), D_MODEL),
        "gla_wa1_f": dense((no, D_MODEL, GLA_GATE_RANK), D_MODEL),
        "gla_wa2_f": dense((no, GLA_GATE_RANK, GLA_KEY_DIM), GLA_GATE_RANK),
        "gla_ba_f": bias((no, GLA_KEY_DIM)),
        "gla_wa1_b": dense((no, D_MODEL, GLA_GATE_RANK), D_MODEL),
        "gla_wa2_b": dense((no, GLA_GATE_RANK, GLA_KEY_DIM), GLA_GATE_RANK),
        "gla_ba_b": bias((no, GLA_KEY_DIM)),
        "gla_o_norm": gain((no, GLA_DV)),
        "gla_wo": dense((no, GLA_VAL_DIM, D_MODEL), GLA_VAL_DIM),
        "ffn_w1": dense((ne, D_MODEL, D_FF), D_MODEL),
        "ffn_w3": dense((ne, D_MODEL, D_FF), D_MODEL),
        "ffn_w2": dense((ne, D_FF, D_MODEL), D_FF),
        "moe_router": dense((no, D_MODEL, N_EXPERTS), D_MODEL),
        "moe_w1": dense((no, N_EXPERTS, D_MODEL, MOE_D_FF), D_MODEL),
        "moe_w3": dense((no, N_EXPERTS, D_MODEL, MOE_D_FF), D_MODEL),
        "moe_w2": dense((no, N_EXPERTS, MOE_D_FF, D_MODEL), MOE_D_FF),
    }


def reference(x_prompt, x_sample, norm_mix, norm_ffn, norm_final,
              attn_wq, attn_wk, attn_wv, attn_q_norm, attn_k_norm, attn_wo,
              gla_wq, gla_wk, gla_wv, gla_wg, gla_wa1_f, gla_wa2_f, gla_ba_f,
              gla_wa1_b, gla_wa2_b, gla_ba_b, gla_o_norm, gla_wo,
              ffn_w1, ffn_w3, ffn_w2,
              moe_router, moe_w1, moe_w3, moe_w2):
    params = (norm_mix, norm_ffn, norm_final,
              attn_wq, attn_wk, attn_wv, attn_q_norm, attn_k_norm, attn_wo,
              gla_wq, gla_wk, gla_wv, gla_wg, gla_wa1_f, gla_wa2_f, gla_ba_f,
              gla_wa1_b, gla_wa2_b, gla_ba_b, gla_o_norm, gla_wo,
              ffn_w1, ffn_w3, ffn_w2,
              moe_router, moe_w1, moe_w3, moe_w2)
    y_prompt = trunk(x_prompt, params)
    y_sample = trunk(x_sample, params)
    return (y_prompt, y_sample)
```

```python
import functools
import math
from typing import NamedTuple

import jax
import jax.numpy as jnp
from jax import lax
from jax.experimental import pallas as pl
from jax.experimental.pallas import tpu as pltpu

F32 = jnp.float32
BF16 = jnp.bfloat16

V7X_VMEM_BYTES = 64 * 1024 * 1024
V7X_LANES = 128
VMEM_CAP_BYTES = V7X_VMEM_BYTES - 8 * 1024 * 1024


class Cfg(NamedTuple):
    d_model: int
    seq_lens: tuple
    grid_w: int
    head_dim: int
    n_q_heads: int
    n_kv_heads: int
    rope_theta: float
    gla_heads: int
    gla_dk: int
    gla_dv: int
    gla_rank: int
    gla_tau: float
    gla_chunk: int
    d_ff: int
    n_experts: int
    moe_d_ff: int
    eps: float


PROD = Cfg(
    d_model=4096, seq_lens=(16384, 2048, 2048, 2048, 2048), grid_w=64,
    head_dim=128, n_q_heads=32, n_kv_heads=8, rope_theta=10000.0,
    gla_heads=4, gla_dk=512, gla_dv=1024, gla_rank=16, gla_tau=16.0, gla_chunk=64,
    d_ff=8192, n_experts=8, moe_d_ff=1024, eps=1e-6)


def _vmem_limit(pipelined_bytes, resident_bytes):
    need = 2 * pipelined_bytes + resident_bytes
    return int(min(VMEM_CAP_BYTES, max(need, 16 * 1024 * 1024)))


def _nbytes(shape, dtype):
    return math.prod(shape) * jnp.dtype(dtype).itemsize


def _tile(n, pref):
    t = min(n, pref)
    assert n % t == 0, (n, pref)
    return t


def _rms(x, eps):
    return x * lax.rsqrt(jnp.mean(x * x, axis=-1, keepdims=True) + eps)


def _rmsnorm_kernel(x_ref, w_ref, o_ref, *, eps):
    o_ref[...] = (_rms(x_ref[...], eps) * w_ref[...]).astype(o_ref.dtype)


def rmsnorm(x, w, out_dtype, cfg):
    n, d = x.shape
    tm = _tile(n, 256)
    return pl.pallas_call(
        functools.partial(_rmsnorm_kernel, eps=cfg.eps),
        grid=(n // tm,),
        in_specs=[pl.BlockSpec((tm, d), lambda i: (i, 0)),
                  pl.BlockSpec((1, d), lambda i: (0, 0))],
        out_specs=pl.BlockSpec((tm, d), lambda i: (i, 0)),
        out_shape=jax.ShapeDtypeStruct((n, d), out_dtype),
        compiler_params=pltpu.CompilerParams(
            dimension_semantics=("parallel",),
            vmem_limit_bytes=_vmem_limit(_nbytes((tm, d), F32) + _nbytes((tm, d), out_dtype),
                                         2 * _nbytes((tm, d), F32))),
        name="rmsnorm",
    )(x, w.reshape(1, d))


def _rmsnorm_router_kernel(x_ref, w_ref, r_ref, h_ref, g_ref, *, eps, n_experts):
    h = _rms(x_ref[...], eps) * w_ref[...]
    h_ref[...] = h.astype(h_ref.dtype)
    logits = jnp.dot(h, r_ref[...], precision=lax.Precision.HIGHEST, preferred_element_type=F32)
    lane = lax.broadcasted_iota(jnp.int32, logits.shape, 1)
    neg = jnp.float32(-jnp.inf)
    logits = jnp.where(lane < n_experts, logits, neg)
    v1 = jnp.max(logits, axis=-1, keepdims=True)
    i1 = jnp.min(jnp.where(logits == v1, lane, V7X_LANES), axis=-1, keepdims=True)
    rest = jnp.where(lane == i1, neg, logits)
    v2 = jnp.max(rest, axis=-1, keepdims=True)
    i2 = jnp.min(jnp.where(rest == v2, lane, V7X_LANES), axis=-1, keepdims=True)
    e2 = jnp.exp(v2 - v1)
    denom = 1.0 + e2
    g_ref[...] = jnp.where(lane == i1, 1.0 / denom, 0.0) + jnp.where(lane == i2, e2 / denom, 0.0)


def rmsnorm_router(x, w, router, cfg):
    n, d = x.shape
    tm = _tile(n, 256)
    r_pad = jnp.zeros((d, V7X_LANES), F32).at[:, :cfg.n_experts].set(router)
    return pl.pallas_call(
        functools.partial(_rmsnorm_router_kernel, eps=cfg.eps, n_experts=cfg.n_experts),
        grid=(n // tm,),
        in_specs=[pl.BlockSpec((tm, d), lambda i: (i, 0)),
                  pl.BlockSpec((1, d), lambda i: (0, 0)),
                  pl.BlockSpec((d, V7X_LANES), lambda i: (0, 0))],
        out_specs=[pl.BlockSpec((tm, d), lambda i: (i, 0)),
                   pl.BlockSpec((tm, V7X_LANES), lambda i: (i, 0))],
        out_shape=[jax.ShapeDtypeStruct((n, d), BF16),
                   jax.ShapeDtypeStruct((n, V7X_LANES), F32)],
        compiler_params=pltpu.CompilerParams(
            dimension_semantics=("parallel",),
            vmem_limit_bytes=_vmem_limit(
                _nbytes((tm, d), F32) + _nbytes((tm, d), BF16) + _nbytes((d, V7X_LANES), F32),
                4 * _nbytes((tm, d), F32))),
        name="rmsnorm_router",
    )(x, w.reshape(1, d), r_pad)


def _mm_kernel(*refs, n_w, n_extra, nk, epilogue):
    a_ref = refs[0]
    w_refs = refs[1:1 + n_w]
    extra_refs = refs[1 + n_w:1 + n_w + n_extra]
    o_ref = refs[1 + n_w + n_extra]
    acc_refs = refs[2 + n_w + n_extra:]
    a = a_ref[...]
    parts = [jnp.dot(a, w[...], preferred_element_type=F32) for w in w_refs]
    if nk == 1:
        epilogue(parts, extra_refs, o_ref)
        return
    k = pl.program_id(2)

    @pl.when(k == 0)
    def _():
        for acc, p in zip(acc_refs, parts):
            acc[...] = p

    @pl.when(jnp.logical_and(k > 0, k < nk - 1))
    def _():
        for acc, p in zip(acc_refs, parts):
            acc[...] += p

    @pl.when(k == nk - 1)
    def _():
        epilogue([acc[...] + p for acc, p in zip(acc_refs, parts)], extra_refs, o_ref)


def matmul(a, ws, epilogue, *, out_cols, out_dtype, tm, tn, tk, extras=(), name):
    m, kdim = a.shape
    n = ws[0].shape[1]
    tm, tn, tk = _tile(m, tm), _tile(n, tn), _tile(kdim, tk)
    nk = kdim // tk
    assert out_cols == n
    in_specs = [pl.BlockSpec((tm, tk), lambda i, j, k: (i, k))]
    in_specs += [pl.BlockSpec((tk, tn), lambda i, j, k: (k, j)) for _ in ws]
    pipelined = _nbytes((tm, tk), a.dtype) + len(ws) * _nbytes((tk, tn), ws[0].dtype)
    for arr, blk, imap in extras:
        in_specs.append(pl.BlockSpec(blk, lambda i, j, k, imap=imap: imap(i, j)))
        pipelined += _nbytes(blk, arr.dtype)
    pipelined += _nbytes((tm, tn), out_dtype)
    acc_bytes = len(ws) * _nbytes((tm, tn), F32)
    scratch = [pltpu.VMEM((tm, tn), F32) for _ in ws] if nk > 1 else []
    return pl.pallas_call(
        functools.partial(_mm_kernel, n_w=len(ws), n_extra=len(extras), nk=nk, epilogue=epilogue),
        grid=(m // tm, n // tn, nk),
        in_specs=in_specs,
        out_specs=pl.BlockSpec((tm, tn), lambda i, j, k: (i, j)),
        out_shape=jax.ShapeDtypeStruct((m, out_cols), out_dtype),
        scratch_shapes=scratch,
        compiler_params=pltpu.CompilerParams(
            dimension_semantics=("parallel", "parallel", "arbitrary"),
            vmem_limit_bytes=_vmem_limit(pipelined, 3 * acc_bytes)),
        name=name,
    )(a, *ws, *[e[0] for e in extras])


def _epi_store(parts, extra_refs, o_ref):
    o_ref[...] = parts[0].astype(o_ref.dtype)


def _epi_residual(parts, extra_refs, o_ref):
    o_ref[...] = (extra_refs[0][...] + parts[0]).astype(o_ref.dtype)


def _silu(x):
    return x * (1.0 / (1.0 + jnp.exp(-x)))


def _epi_swiglu(parts, extra_refs, o_ref):
    o_ref[...] = (_silu(parts[0]) * parts[1]).astype(o_ref.dtype)


def _epi_swiglu_gated(parts, extra_refs, o_ref, *, cols_per_expert, tn):
    gates = extra_refs[0][...]
    e = (pl.program_id(1) * tn) // cols_per_expert
    lane = lax.broadcasted_iota(jnp.int32, gates.shape, 1)
    g = jnp.sum(jnp.where(lane == e, gates, 0.0), axis=-1, keepdims=True)
    o_ref[...] = (_silu(parts[0]) * parts[1] * g).astype(o_ref.dtype)


def _epi_qkv(parts, extra_refs, o_ref, *, n_q_blocks, n_k_blocks, head_dim, eps):
    acc = parts[0]
    cos_ref, sin_lo_ref, sin_hi_ref, qw_ref, kw_ref = extra_refs
    j = pl.program_id(1)
    heads = acc.shape[1] // head_dim

    def norm_rope(w):
        cos, sin_lo, sin_hi = cos_ref[...], sin_lo_ref[...], sin_hi_ref[...]
        for g in range(heads):
            sl = slice(g * head_dim, (g + 1) * head_dim)
            y = _rms(acc[:, sl], eps) * w
            r = (y * cos + pltpu.roll(y, head_dim - head_dim // 4, 1) * sin_lo
                 + pltpu.roll(y, head_dim // 4, 1) * sin_hi)
            o_ref[:, sl] = r.astype(o_ref.dtype)

    @pl.when(j < n_q_blocks)
    def _():
        norm_rope(qw_ref[...])

    @pl.when(jnp.logical_and(j >= n_q_blocks, j < n_q_blocks + n_k_blocks))
    def _():
        norm_rope(kw_ref[...])

    @pl.when(j >= n_q_blocks + n_k_blocks)
    def _():
        o_ref[...] = acc.astype(o_ref.dtype)


def _epi_gla_gate(parts, extra_refs, o_ref, *, eps):
    o2_ref, w_ref = extra_refs
    o = o2_ref[0] + o2_ref[1]
    o_ref[...] = (_rms(o, eps) * w_ref[...] * _silu(parts[0])).astype(o_ref.dtype)


def _rope_tables(cfg):
    hd = cfg.head_dim
    axis_dim = hd // 2
    inv_freq = cfg.rope_theta ** (-jnp.arange(0, axis_dim, 2, dtype=F32) / axis_dim)
    tabs = []
    for t in cfg.seq_lens:
        pos = jnp.arange(t)
        ang_r = (pos // cfg.grid_w).astype(F32)[:, None] * inv_freq
        ang_c = (pos % cfg.grid_w).astype(F32)[:, None] * inv_freq
        tabs.append(jnp.concatenate([ang_r, ang_r, ang_c, ang_c], axis=-1))
    ang = jnp.concatenate(tabs, axis=0)
    cos, sin = jnp.cos(ang), jnp.sin(ang)
    first_half = (jnp.arange(hd) % (hd // 2)) < (hd // 4)
    sin_lo = jnp.where(first_half, -sin, 0.0)
    sin_hi = jnp.where(first_half, 0.0, sin)
    return cos, sin_lo, sin_hi


def _flash_kernel(q_ref, k_ref, v_ref, *rest, tk, group, head_dim):
    o_ref = rest[-1]
    t = k_ref.shape[0]
    tq = q_ref.shape[0]
    qs = [q_ref[:, g * head_dim:(g + 1) * head_dim] for g in range(group)]

    def body(kb, carry):
        off = pl.multiple_of(kb * tk, tk)
        kblk = k_ref[pl.ds(off, tk), :]
        vblk = v_ref[pl.ds(off, tk), :]
        out = []
        for g in range(group):
            m_prev, l_prev, acc_prev = carry[g]
            s = lax.dot_general(qs[g], kblk, (((1,), (1,)), ((), ())), preferred_element_type=F32)
            m_new = jnp.maximum(m_prev, jnp.max(s, axis=-1, keepdims=True))
            p = jnp.exp2(s - m_new)
            alpha = jnp.exp2(m_prev - m_new)
            l_new = alpha * l_prev + jnp.sum(p, axis=-1, keepdims=True)
            acc_new = alpha * acc_prev + jnp.dot(p.astype(BF16), vblk, preferred_element_type=F32)
            out.append((m_new, l_new, acc_new))
        return tuple(out)

    init = tuple((jnp.full((tq, 1), -jnp.inf, F32), jnp.zeros((tq, 1), F32),
                  jnp.zeros((tq, head_dim), F32)) for _ in range(group))
    final = lax.fori_loop(0, t // tk, body, init)
    for g in range(group):
        _, l, acc = final[g]
        o_ref[:, g * head_dim:(g + 1) * head_dim] = (acc * (1.0 / l)).astype(o_ref.dtype)


def _flash_call(qkv, prev_out, *, row0, t, n_seq, cfg):
    n = qkv.shape[0]
    hd, group = cfg.head_dim, cfg.n_q_heads // cfg.n_kv_heads
    tq = _tile(t, 256)
    tk = _tile(t, 512)
    assert row0 % t == 0 and row0 % tq == 0
    qb0, sb0 = row0 // tq, row0 // t
    k_col0 = cfg.n_q_heads
    v_col0 = cfg.n_q_heads + cfg.n_kv_heads
    in_specs = [
        pl.BlockSpec((tq, group * hd), lambda b, h, i: (qb0 + b * (t // tq) + i, h)),
        pl.BlockSpec((t, hd), lambda b, h, i: (sb0 + b, k_col0 + h)),
        pl.BlockSpec((t, hd), lambda b, h, i: (sb0 + b, v_col0 + h)),
    ]
    args = [qkv, qkv, qkv]
    aliases = {}
    if prev_out is not None:
        in_specs.append(pl.BlockSpec(memory_space=pl.ANY))
        args.append(prev_out)
        aliases = {3: 0}
    pipelined = 2 * _nbytes((tq, group * hd), BF16) + 2 * _nbytes((t, hd), BF16)
    temps = 6 * group * _nbytes((tq, tk), F32)
    return pl.pallas_call(
        functools.partial(_flash_kernel, tk=tk, group=group, head_dim=hd),
        grid=(n_seq, cfg.n_kv_heads, t // tq),
        in_specs=in_specs,
        out_specs=pl.BlockSpec((tq, group * hd), lambda b, h, i: (qb0 + b * (t // tq) + i, h)),
        out_shape=jax.ShapeDtypeStruct((n, cfg.n_q_heads * hd), BF16),
        input_output_aliases=aliases,
        compiler_params=pltpu.CompilerParams(
            dimension_semantics=("parallel", "parallel", "arbitrary"),
            vmem_limit_bytes=_vmem_limit(pipelined, temps)),
        name=f"flash_t{t}",
    )(*args)


def attention(qkv, cfg):
    out, row0, idx = None, 0, 0
    lens = cfg.seq_lens
    while idx < len(lens):
        t, n_seq = lens[idx], 1
        while idx + n_seq < len(lens) and lens[idx + n_seq] == t:
            n_seq += 1
        out = _flash_call(qkv, out, row0=row0, t=t, n_seq=n_seq, cfg=cfg)
        row0 += t * n_seq
        idx += n_seq
    return out


def _log_sigmoid(x):
    return jnp.minimum(x, 0.0) - jnp.log1p(jnp.exp(-jnp.abs(x)))


def _gla_kernel(starts_ref, ends_ref, q_ref, k_ref, v_ref, a_ref, wa2_ref, ba_ref, o_ref, st_ref,
                *, chunk, n_chunks, n_blocks, tau, q_scale):
    d = pl.program_id(1)
    i = pl.program_id(2)
    fwd = d == 0
    blk = jnp.where(fwd, i, n_blocks - 1 - i)
    boundary = jnp.where(fwd, starts_ref[blk], ends_ref[blk])

    @pl.when(boundary == 1)
    def _():
        st_ref[...] = jnp.zeros_like(st_ref)

    row = lax.broadcasted_iota(jnp.int32, (chunk, chunk), 0)
    col = lax.broadcasted_iota(jnp.int32, (chunk, chunk), 1)
    dist = (row - col) * (1 - 2 * d)
    tri = jnp.where(dist >= 0, 1.0, 0.0).astype(F32)
    mask = dist >= d
    log_q_scale = math.log(q_scale)

    for c in range(n_chunks):
        cc = jnp.where(fwd, c, n_chunks - 1 - c)
        rows = pl.ds(pl.multiple_of(cc * chunk, chunk), chunk)
        q = q_ref[rows, :]
        k = k_ref[rows, :]
        v = v_ref[rows, :]
        logits = jnp.dot(a_ref[rows, :].astype(BF16), wa2_ref[0], preferred_element_type=F32)
        g = _log_sigmoid(logits + ba_ref[0]) / tau
        bcum = jnp.dot(tri, g, precision=lax.Precision.HIGHEST, preferred_element_type=F32)
        b_all = jnp.sum(g, axis=0, keepdims=True)
        q_t = (q * jnp.exp(bcum + log_q_scale)).astype(BF16)
        k_t = (k * jnp.exp(-bcum)).astype(BF16)
        k_dec = (k * jnp.exp(b_all - bcum)).astype(BF16)
        attn = lax.dot_general(q_t, k_t, (((1,), (1,)), ((), ())), preferred_element_type=F32)
        attn = jnp.where(mask, attn, 0.0).astype(BF16)
        st = st_ref[...]
        o = jnp.dot(attn, v, preferred_element_type=F32)
        o += lax.dot_general(q_t, st.astype(BF16), (((1,), (1,)), ((), ())), preferred_element_type=F32)
        o_ref[0, rows, :] = o
        st_ref[...] = st * jnp.exp(b_all) + lax.dot_general(
            v, k_dec, (((0,), (0,)), ((), ())), preferred_element_type=F32)


def gla_scan(qk, v, a, wa2, ba, cfg):
    n = qk.shape[0]
    nh, dk, dv, chunk = cfg.gla_heads, cfg.gla_dk, cfg.gla_dv, cfg.gla_chunk
    rblk = _tile(math.gcd(*cfg.seq_lens), 256)
    n_blocks = n // rblk
    starts, ends, row = [0] * n_blocks, [0] * n_blocks, 0
    for t in cfg.seq_lens:
        starts[row // rblk] = 1
        row += t
        ends[row // rblk - 1] = 1
    starts = jnp.asarray(starts, jnp.int32)
    ends = jnp.asarray(ends, jnp.int32)

    def rb(d, i):
        return i + d * (n_blocks - 1 - 2 * i)

    grid_spec = pltpu.PrefetchScalarGridSpec(
        num_scalar_prefetch=2,
        grid=(nh, 2, n_blocks),
        in_specs=[
            pl.BlockSpec((rblk, dk), lambda h, d, i, s, e: (rb(d, i), h)),
            pl.BlockSpec((rblk, dk), lambda h, d, i, s, e: (rb(d, i), nh + h)),
            pl.BlockSpec((rblk, dv), lambda h, d, i, s, e: (rb(d, i), h)),
            pl.BlockSpec((rblk, V7X_LANES), lambda h, d, i, s, e: (rb(d, i), 0)),
            pl.BlockSpec((1, V7X_LANES, dk), lambda h, d, i, s, e: (d, 0, h)),
            pl.BlockSpec((1, 1, dk), lambda h, d, i, s, e: (d, 0, h)),
        ],
        out_specs=pl.BlockSpec((1, rblk, dv), lambda h, d, i, s, e: (d, rb(d, i), h)),
        scratch_shapes=[pltpu.VMEM((dv, dk), F32)],
    )
    pipelined = (2 * _nbytes((rblk, dk), F32) + _nbytes((rblk, dv), BF16) + _nbytes((rblk, V7X_LANES), F32)
                 + _nbytes((V7X_LANES, dk), BF16) + _nbytes((rblk, dv), F32))
    return pl.pallas_call(
        functools.partial(_gla_kernel, chunk=chunk, n_chunks=rblk // chunk, n_blocks=n_blocks,
                          tau=cfg.gla_tau, q_scale=dk ** -0.5),
        grid_spec=grid_spec,
        out_shape=jax.ShapeDtypeStruct((2, n, nh * dv), F32),
        compiler_params=pltpu.CompilerParams(
            dimension_semantics=("parallel", "arbitrary", "arbitrary"),
            vmem_limit_bytes=_vmem_limit(pipelined, 4 * _nbytes((dv, dk), F32))),
        name="gla_scan",
    )(starts, ends, qk, qk, v, a, wa2, ba)


def _trunk(x, p, cfg):
    d = cfg.d_model
    hd = cfg.head_dim
    n = x.shape[0]
    bf = lambda w: w.astype(BF16)
    TM = 1024

    h = rmsnorm(x, p["norm_mix"][0], BF16, cfg)
    w_qkv = bf(jnp.concatenate([p["attn_wq"][0], p["attn_wk"][0], p["attn_wv"][0]], axis=1))
    cos, sin_lo, sin_hi = _rope_tables(cfg)
    tn_qkv = math.gcd(4 * hd, cfg.n_kv_heads * hd)
    tab = lambda arr: (arr, (TM if n >= TM else n, hd), lambda i, j: (i, 0))
    vec = lambda arr: (arr.reshape(1, -1), (1, arr.size), lambda i, j: (0, 0))
    q_scale = hd ** -0.5 * math.log2(math.e)
    qkv = matmul(
        h, [w_qkv],
        functools.partial(_epi_qkv, n_q_blocks=cfg.n_q_heads * hd // tn_qkv,
                          n_k_blocks=cfg.n_kv_heads * hd // tn_qkv, head_dim=hd, eps=cfg.eps),
        out_cols=w_qkv.shape[1], out_dtype=BF16, tm=TM, tn=tn_qkv, tk=d,
        extras=[tab(cos), tab(sin_lo), tab(sin_hi),
                vec(p["attn_q_norm"][0] * q_scale), vec(p["attn_k_norm"][0])],
        name="qkv_proj")
    o = attention(qkv, cfg)
    res = lambda arr, tn: (arr, (TM if n >= TM else n, tn), lambda i, j: (i, j))
    x = matmul(o, [bf(p["attn_wo"][0])], _epi_residual, out_cols=d, out_dtype=F32,
               tm=TM, tn=1024, tk=o.shape[1], extras=[res(x, 1024)], name="attn_out")

    h = rmsnorm(x, p["norm_ffn"][0], BF16, cfg)
    hid = matmul(h, [bf(p["ffn_w1"][0]), bf(p["ffn_w3"][0])], _epi_swiglu,
                 out_cols=cfg.d_ff, out_dtype=BF16, tm=TM, tn=512, tk=d, name="ffn_up")
    x = matmul(hid, [bf(p["ffn_w2"][0])], _epi_residual, out_cols=d, out_dtype=F32,
               tm=TM, tn=1024, tk=2048, extras=[res(x, 1024)], name="ffn_down")

    nh, dk, dv, rank = cfg.gla_heads, cfg.gla_dk, cfg.gla_dv, cfg.gla_rank
    h = rmsnorm(x, p["norm_mix"][1], BF16, cfg)
    qk = matmul(h, [bf(jnp.concatenate([p["gla_wq"][0], p["gla_wk"][0]], axis=1))], _epi_store,
                out_cols=2 * nh * dk, out_dtype=F32, tm=TM, tn=1024, tk=d, name="gla_qk")
    v = matmul(h, [bf(p["gla_wv"][0])], _epi_store, out_cols=nh * dv, out_dtype=BF16,
               tm=TM, tn=1024, tk=d, name="gla_v")
    wa1 = jnp.zeros((d, V7X_LANES), F32)
    wa1 = wa1.at[:, :rank].set(p["gla_wa1_f"][0]).at[:, rank:2 * rank].set(p["gla_wa1_b"][0])
    a = matmul(h, [bf(wa1)], _epi_store, out_cols=V7X_LANES, out_dtype=F32,
               tm=TM, tn=V7X_LANES, tk=d, name="gla_gate_lowrank")
    wa2 = jnp.zeros((2, V7X_LANES, nh * dk), F32)
    wa2 = wa2.at[0, :rank].set(p["gla_wa2_f"][0]).at[1, rank:2 * rank].set(p["gla_wa2_b"][0])
    ba = jnp.stack([p["gla_ba_f"][0], p["gla_ba_b"][0]]).reshape(2, 1, nh * dk)
    o2 = gla_scan(qk, v, a, bf(wa2), ba, cfg)
    gated = matmul(
        h, [bf(p["gla_wg"][0])], functools.partial(_epi_gla_gate, eps=cfg.eps),
        out_cols=nh * dv, out_dtype=BF16, tm=TM // 2, tn=dv, tk=d,
        extras=[(o2, (2, TM // 2 if n >= TM // 2 else n, dv), lambda i, j: (0, i, j)),
                (jnp.tile(p["gla_o_norm"][0], nh).reshape(1, nh * dv), (1, dv), lambda i, j: (0, j))],
        name="gla_gate")
    x = matmul(gated, [bf(p["gla_wo"][0])], _epi_residual, out_cols=d, out_dtype=F32,
               tm=TM, tn=1024, tk=nh * dv, extras=[res(x, 1024)], name="gla_out")

    ne, eff = cfg.n_experts, cfg.moe_d_ff
    h, gates = rmsnorm_router(x, p["norm_ffn"][1], p["moe_router"][0], cfg)
    w1 = bf(p["moe_w1"][0]).transpose(1, 0, 2).reshape(d, ne * eff)
    w3 = bf(p["moe_w3"][0]).transpose(1, 0, 2).reshape(d, ne * eff)
    w2 = bf(p["moe_w2"][0]).reshape(ne * eff, d)
    tn_moe = min(512, eff)
    hid = matmul(h, [w1, w3], functools.partial(_epi_swiglu_gated, cols_per_expert=eff, tn=tn_moe),
                 out_cols=ne * eff, out_dtype=BF16, tm=TM, tn=tn_moe, tk=d,
                 extras=[(gates, (TM if n >= TM else n, V7X_LANES), lambda i, j: (i, 0))],
                 name="moe_up")
    x = matmul(hid, [w2], _epi_residual, out_cols=d, out_dtype=F32,
               tm=TM, tn=1024, tk=2048, extras=[res(x, 1024)], name="moe_down")

    return rmsnorm(x, p["norm_final"], F32, cfg)


def kernel(x_prompt, x_sample, norm_mix, norm_ffn, norm_final, attn_wq, attn_wk, attn_wv, attn_q_norm, attn_k_norm, attn_wo, gla_wq, gla_wk, gla_wv, gla_wg, gla_wa1_f, gla_wa2_f, gla_ba_f, gla_wa1_b, gla_wa2_b, gla_ba_b, gla_o_norm, gla_wo, ffn_w1, ffn_w3, ffn_w2, moe_router, moe_w1, moe_w3, moe_w2):
    cfg = PROD
    d = cfg.d_model
    params = dict(
        norm_mix=norm_mix, norm_ffn=norm_ffn, norm_final=norm_final,
        attn_wq=attn_wq, attn_wk=attn_wk, attn_wv=attn_wv, attn_q_norm=attn_q_norm,
        attn_k_norm=attn_k_norm, attn_wo=attn_wo,
        gla_wq=gla_wq, gla_wk=gla_wk, gla_wv=gla_wv, gla_wg=gla_wg,
        gla_wa1_f=gla_wa1_f, gla_wa2_f=gla_wa2_f, gla_ba_f=gla_ba_f,
        gla_wa1_b=gla_wa1_b, gla_wa2_b=gla_wa2_b, gla_ba_b=gla_ba_b,
        gla_o_norm=gla_o_norm, gla_wo=gla_wo,
        ffn_w1=ffn_w1, ffn_w3=ffn_w3, ffn_w2=ffn_w2,
        moe_router=moe_router, moe_w1=moe_w1, moe_w3=moe_w3, moe_w2=moe_w2)
    n_prompt = x_prompt.shape[0] * x_prompt.shape[1]
    x = jnp.concatenate([x_prompt.reshape(-1, d), x_sample.reshape(-1, d)], axis=0)
    y = _trunk(x, params, cfg)
    return (y[:n_prompt].reshape(x_prompt.shape), y[n_prompt:].reshape(x_sample.shape))
```

```python
import functools
import math
from typing import NamedTuple

import jax
import jax.numpy as jnp
from jax import lax
from jax.experimental import pallas as pl
from jax.experimental.pallas import tpu as pltpu

F32 = jnp.float32
BF16 = jnp.bfloat16

V7X_VMEM_BYTES = 64 * 1024 * 1024
V7X_LANES = 128
VMEM_CAP_BYTES = V7X_VMEM_BYTES - 8 * 1024 * 1024


class Cfg(NamedTuple):
    d_model: int
    seq_lens: tuple
    grid_w: int
    head_dim: int
    n_q_heads: int
    n_kv_heads: int
    rope_theta: float
    gla_heads: int
    gla_dk: int
    gla_dv: int
    gla_rank: int
    gla_tau: float
    gla_chunk: int
    d_ff: int
    n_experts: int
    moe_d_ff: int
    eps: float


PROD = Cfg(
    d_model=4096, seq_lens=(16384, 2048, 2048, 2048, 2048), grid_w=64,
    head_dim=128, n_q_heads=32, n_kv_heads=8, rope_theta=10000.0,
    gla_heads=4, gla_dk=512, gla_dv=1024, gla_rank=16, gla_tau=16.0, gla_chunk=64,
    d_ff=8192, n_experts=8, moe_d_ff=1024, eps=1e-6)


def _vmem_limit(pipelined_bytes, resident_bytes):
    need = 2 * pipelined_bytes + resident_bytes
    return int(min(VMEM_CAP_BYTES, max(need, 16 * 1024 * 1024)))


def _nbytes(shape, dtype):
    return math.prod(shape) * jnp.dtype(dtype).itemsize


def _tile(n, pref):
    t = min(n, pref)
    assert n % t == 0, (n, pref)
    return t


def _rms(x, eps):
    return x * lax.rsqrt(jnp.mean(x * x, axis=-1, keepdims=True) + eps)


def _rmsnorm_kernel(x_ref, w_ref, o_ref, *, eps):
    o_ref[...] = (_rms(x_ref[...], eps) * w_ref[...]).astype(o_ref.dtype)


def rmsnorm(x, w, out_dtype, cfg):
    n, d = x.shape
    tm = _tile(n, 256)
    return pl.pallas_call(
        functools.partial(_rmsnorm_kernel, eps=cfg.eps),
        grid=(n // tm,),
        in_specs=[pl.BlockSpec((tm, d), lambda i: (i, 0)),
                  pl.BlockSpec((1, d), lambda i: (0, 0))],
        out_specs=pl.BlockSpec((tm, d), lambda i: (i, 0)),
        out_shape=jax.ShapeDtypeStruct((n, d), out_dtype),
        compiler_params=pltpu.CompilerParams(
            dimension_semantics=("parallel",),
            vmem_limit_bytes=_vmem_limit(_nbytes((tm, d), F32) + _nbytes((tm, d), out_dtype),
                                         2 * _nbytes((tm, d), F32))),
        name="rmsnorm",
    )(x, w.reshape(1, d))


def _rmsnorm_router_kernel(x_ref, w_ref, r_ref, h_ref, g_ref, *, eps, n_experts):
    h = _rms(x_ref[...], eps) * w_ref[...]
    h_ref[...] = h.astype(h_ref.dtype)
    logits = jnp.dot(h, r_ref[...], precision=lax.Precision.HIGHEST, preferred_element_type=F32)
    lane = lax.broadcasted_iota(jnp.int32, logits.shape, 1)
    neg = jnp.float32(-jnp.inf)
    logits = jnp.where(lane < n_experts, logits, neg)
    v1 = jnp.max(logits, axis=-1, keepdims=True)
    i1 = jnp.min(jnp.where(logits == v1, lane, V7X_LANES), axis=-1, keepdims=True)
    rest = jnp.where(lane == i1, neg, logits)
    v2 = jnp.max(rest, axis=-1, keepdims=True)
    i2 = jnp.min(jnp.where(rest == v2, lane, V7X_LANES), axis=-1, keepdims=True)
    e2 = jnp.exp(v2 - v1)
    denom = 1.0 + e2
    g_ref[...] = jnp.where(lane == i1, 1.0 / denom, 0.0) + jnp.where(lane == i2, e2 / denom, 0.0)


def rmsnorm_router(x, w, router, cfg):
    n, d = x.shape
    tm = _tile(n, 256)
    r_pad = jnp.zeros((d, V7X_LANES), F32).at[:, :cfg.n_experts].set(router)
    return pl.pallas_call(
        functools.partial(_rmsnorm_router_kernel, eps=cfg.eps, n_experts=cfg.n_experts),
        grid=(n // tm,),
        in_specs=[pl.BlockSpec((tm, d), lambda i: (i, 0)),
                  pl.BlockSpec((1, d), lambda i: (0, 0)),
                  pl.BlockSpec((d, V7X_LANES), lambda i: (0, 0))],
        out_specs=[pl.BlockSpec((tm, d), lambda i: (i, 0)),
                   pl.BlockSpec((tm, V7X_LANES), lambda i: (i, 0))],
        out_shape=[jax.ShapeDtypeStruct((n, d), BF16),
                   jax.ShapeDtypeStruct((n, V7X_LANES), F32)],
        compiler_params=pltpu.CompilerParams(
            dimension_semantics=("parallel",),
            vmem_limit_bytes=_vmem_limit(
                _nbytes((tm, d), F32) + _nbytes((tm, d), BF16) + _nbytes((d, V7X_LANES), F32),
                4 * _nbytes((tm, d), F32))),
        name="rmsnorm_router",
    )(x, w.reshape(1, d), r_pad)


def _mm_kernel(*refs, n_w, n_extra, nk, epilogue):
    a_ref = refs[0]
    w_refs = refs[1:1 + n_w]
    extra_refs = refs[1 + n_w:1 + n_w + n_extra]
    o_ref = refs[1 + n_w + n_extra]
    acc_refs = refs[2 + n_w + n_extra:]
    a = a_ref[...]
    parts = [jnp.dot(a, w[...], preferred_element_type=F32) for w in w_refs]
    if nk == 1:
        epilogue(parts, extra_refs, o_ref)
        return
    k = pl.program_id(2)

    @pl.when(k == 0)
    def _():
        for acc, p in zip(acc_refs, parts):
            acc[...] = p

    @pl.when(jnp.logical_and(k > 0, k < nk - 1))
    def _():
        for acc, p in zip(acc_refs, parts):
            acc[...] += p

    @pl.when(k == nk - 1)
    def _():
        epilogue([acc[...] + p for acc, p in zip(acc_refs, parts)], extra_refs, o_ref)


def matmul(a, ws, epilogue, *, out_cols, out_dtype, tm, tn, tk, extras=(), name):
    m, kdim = a.shape
    n = ws[0].shape[1]
    tm, tn, tk = _tile(m, tm), _tile(n, tn), _tile(kdim, tk)
    nk = kdim // tk
    assert out_cols == n
    in_specs = [pl.BlockSpec((tm, tk), lambda i, j, k: (i, k))]
    in_specs += [pl.BlockSpec((tk, tn), lambda i, j, k: (k, j)) for _ in ws]
    pipelined = _nbytes((tm, tk), a.dtype) + len(ws) * _nbytes((tk, tn), ws[0].dtype)
    for arr, blk, imap in extras:
        in_specs.append(pl.BlockSpec(blk, lambda i, j, k, imap=imap: imap(i, j)))
        pipelined += _nbytes(blk, arr.dtype)
    pipelined += _nbytes((tm, tn), out_dtype)
    acc_bytes = len(ws) * _nbytes((tm, tn), F32)
    scratch = [pltpu.VMEM((tm, tn), F32) for _ in ws] if nk > 1 else []
    return pl.pallas_call(
        functools.partial(_mm_kernel, n_w=len(ws), n_extra=len(extras), nk=nk, epilogue=epilogue),
        grid=(m // tm, n // tn, nk),
        in_specs=in_specs,
        out_specs=pl.BlockSpec((tm, tn), lambda i, j, k: (i, j)),
        out_shape=jax.ShapeDtypeStruct((m, out_cols), out_dtype),
        scratch_shapes=scratch,
        compiler_params=pltpu.CompilerParams(
            dimension_semantics=("parallel", "parallel", "arbitrary"),
            vmem_limit_bytes=_vmem_limit(pipelined, 3 * acc_bytes)),
        name=name,
    )(a, *ws, *[e[0] for e in extras])


def _epi_store(parts, extra_refs, o_ref):
    o_ref[...] = parts[0].astype(o_ref.dtype)


def _epi_residual(parts, extra_refs, o_ref):
    o_ref[...] = (extra_refs[0][...] + parts[0]).astype(o_ref.dtype)


def _silu(x):
    return x * (1.0 / (1.0 + jnp.exp(-x)))


def _epi_swiglu(parts, extra_refs, o_ref):
    o_ref[...] = (_silu(parts[0]) * parts[1]).astype(o_ref.dtype)


def _epi_swiglu_gated(parts, extra_refs, o_ref, *, cols_per_expert, tn):
    gates = extra_refs[0][...]
    e = (pl.program_id(1) * tn) // cols_per_expert
    lane = lax.broadcasted_iota(jnp.int32, gates.shape, 1)
    g = jnp.sum(jnp.where(lane == e, gates, 0.0), axis=-1, keepdims=True)
    o_ref[...] = (_silu(parts[0]) * parts[1] * g).astype(o_ref.dtype)


def _epi_qkv(parts, extra_refs, o_ref, *, n_q_blocks, n_k_blocks, head_dim, eps):
    acc = parts[0]
    cos_ref, sin_lo_ref, sin_hi_ref, qw_ref, kw_ref = extra_refs
    j = pl.program_id(1)
    heads = acc.shape[1] // head_dim

    def norm_rope(w):
        cos, sin_lo, sin_hi = cos_ref[...], sin_lo_ref[...], sin_hi_ref[...]
        for g in range(heads):
            sl = slice(g * head_dim, (g + 1) * head_dim)
            y = _rms(acc[:, sl], eps) * w
            r = (y * cos + pltpu.roll(y, head_dim - head_dim // 4, 1) * sin_lo
                 + pltpu.roll(y, head_dim // 4, 1) * sin_hi)
            o_ref[:, sl] = r.astype(o_ref.dtype)

    @pl.when(j < n_q_blocks)
    def _():
        norm_rope(qw_ref[...])

    @pl.when(jnp.logical_and(j >= n_q_blocks, j < n_q_blocks + n_k_blocks))
    def _():
        norm_rope(kw_ref[...])

    @pl.when(j >= n_q_blocks + n_k_blocks)
    def _():
        o_ref[...] = acc.astype(o_ref.dtype)


def _epi_gla_gate(parts, extra_refs, o_ref, *, eps):
    o2_ref, w_ref = extra_refs
    o = o2_ref[0] + o2_ref[1]
    o_ref[...] = (_rms(o, eps) * w_ref[...] * _silu(parts[0])).astype(o_ref.dtype)


def _rope_tables(cfg):
    hd = cfg.head_dim
    axis_dim = hd // 2
    inv_freq = cfg.rope_theta ** (-jnp.arange(0, axis_dim, 2, dtype=F32) / axis_dim)
    tabs = []
    for t in cfg.seq_lens:
        pos = jnp.arange(t)
        ang_r = (pos // cfg.grid_w).astype(F32)[:, None] * inv_freq
        ang_c = (pos % cfg.grid_w).astype(F32)[:, None] * inv_freq
        tabs.append(jnp.concatenate([ang_r, ang_r, ang_c, ang_c], axis=-1))
    ang = jnp.concatenate(tabs, axis=0)
    cos, sin = jnp.cos(ang), jnp.sin(ang)
    first_half = (jnp.arange(hd) % (hd // 2)) < (hd // 4)
    sin_lo = jnp.where(first_half, -sin, 0.0)
    sin_hi = jnp.where(first_half, 0.0, sin)
    return cos, sin_lo, sin_hi


def _flash_kernel(q_ref, k_ref, v_ref, *rest, tk, group, head_dim):
    o_ref = rest[-1]
    t = k_ref.shape[0]
    tq = q_ref.shape[0]
    qs = [q_ref[:, g * head_dim:(g + 1) * head_dim] for g in range(group)]

    def body(kb, carry):
        off = pl.multiple_of(kb * tk, tk)
        kblk = k_ref[pl.ds(off, tk), :]
        vblk = v_ref[pl.ds(off, tk), :]
        out = []
        for g in range(group):
            m_prev, l_prev, acc_prev = carry[g]
            s = lax.dot_general(qs[g], kblk, (((1,), (1,)), ((), ())), preferred_element_type=F32)
            m_new = jnp.maximum(m_prev, jnp.max(s, axis=-1, keepdims=True))
            p = jnp.exp2(s - m_new)
            alpha = jnp.exp2(m_prev - m_new)
            l_new = alpha * l_prev + jnp.sum(p, axis=-1, keepdims=True)
            acc_new = alpha * acc_prev + jnp.dot(p.astype(BF16), vblk, preferred_element_type=F32)
            out.append((m_new, l_new, acc_new))
        return tuple(out)

    init = tuple((jnp.full((tq, 1), -jnp.inf, F32), jnp.zeros((tq, 1), F32),
                  jnp.zeros((tq, head_dim), F32)) for _ in range(group))
    final = lax.fori_loop(0, t // tk, body, init)
    for g in range(group):
        _, l, acc = final[g]
        o_ref[:, g * head_dim:(g + 1) * head_dim] = (acc * (1.0 / l)).astype(o_ref.dtype)


def _flash_call(qkv, prev_out, *, row0, t, n_seq, cfg):
    n = qkv.shape[0]
    hd, group = cfg.head_dim, cfg.n_q_heads // cfg.n_kv_heads
    tq = _tile(t, 256)
    tk = _tile(t, 512)
    assert row0 % t == 0 and row0 % tq == 0
    qb0, sb0 = row0 // tq, row0 // t
    k_col0 = cfg.n_q_heads
    v_col0 = cfg.n_q_heads + cfg.n_kv_heads
    in_specs = [
        pl.BlockSpec((tq, group * hd), lambda b, h, i: (qb0 + b * (t // tq) + i, h)),
        pl.BlockSpec((t, hd), lambda b, h, i: (sb0 + b, k_col0 + h)),
        pl.BlockSpec((t, hd), lambda b, h, i: (sb0 + b, v_col0 + h)),
    ]
    args = [qkv, qkv, qkv]
    aliases = {}
    if prev_out is not None:
        in_specs.append(pl.BlockSpec(memory_space=pl.ANY))
        args.append(prev_out)
        aliases = {3: 0}
    pipelined = 2 * _nbytes((tq, group * hd), BF16) + 2 * _nbytes((t, hd), BF16)
    temps = 6 * group * _nbytes((tq, tk), F32)
    return pl.pallas_call(
        functools.partial(_flash_kernel, tk=tk, group=group, head_dim=hd),
        grid=(n_seq, cfg.n_kv_heads, t // tq),
        in_specs=in_specs,
        out_specs=pl.BlockSpec((tq, group * hd), lambda b, h, i: (qb0 + b * (t // tq) + i, h)),
        out_shape=jax.ShapeDtypeStruct((n, cfg.n_q_heads * hd), BF16),
        input_output_aliases=aliases,
        compiler_params=pltpu.CompilerParams(
            dimension_semantics=("parallel", "parallel", "arbitrary"),
            vmem_limit_bytes=_vmem_limit(pipelined, temps)),
        name=f"flash_t{t}",
    )(*args)


def _flash_bounded_kernel(q_ref, kt_ref, v_ref, *rest, group, head_dim):
    o_ref, l_ref, acc_ref = rest[-3:]
    tq = q_ref.shape[0]
    n_kb, tk = kt_ref.shape[1], kt_ref.shape[3]
    q = jnp.concatenate([q_ref[:, g * head_dim:(g + 1) * head_dim] for g in range(group)], axis=0)
    l_ref[...] = jnp.zeros_like(l_ref)
    acc_ref[...] = jnp.zeros_like(acc_ref)

    def body(kb, carry):
        s = jnp.dot(q, kt_ref[0, kb], preferred_element_type=F32)
        p = jnp.exp2(s)
        part = p[:, :V7X_LANES]
        for c in range(1, tk // V7X_LANES):
            part = part + p[:, c * V7X_LANES:(c + 1) * V7X_LANES]
        l_ref[...] += part
        vblk = v_ref[pl.ds(pl.multiple_of(kb * tk, tk), tk), :]
        acc_ref[...] += jnp.dot(p.astype(BF16), vblk, preferred_element_type=F32)
        return carry

    lax.fori_loop(0, n_kb, body, 0)
    o = acc_ref[...] * (1.0 / jnp.sum(l_ref[...], axis=-1, keepdims=True))
    for g in range(group):
        o_ref[:, g * head_dim:(g + 1) * head_dim] = o[g * tq:(g + 1) * tq].astype(o_ref.dtype)


def _flash_bounded_call(qkv, kt, prev_out, *, row0, t, n_seq, tk, cfg):
    n = qkv.shape[0]
    hd, group = cfg.head_dim, cfg.n_q_heads // cfg.n_kv_heads
    tq = _tile(t, 256)
    assert row0 % t == 0 and row0 % tq == 0 and t % tk == 0
    qb0, sb0 = row0 // tq, row0 // t
    v_col0 = cfg.n_q_heads + cfg.n_kv_heads
    in_specs = [
        pl.BlockSpec((tq, group * hd), lambda b, h, i: (qb0 + b * (t // tq) + i, h)),
        pl.BlockSpec((1, t // tk, hd, tk), lambda b, h, i: (h, sb0 + b, 0, 0)),
        pl.BlockSpec((t, hd), lambda b, h, i: (sb0 + b, v_col0 + h)),
    ]
    args = [qkv, kt, qkv]
    aliases = {}
    if prev_out is not None:
        in_specs.append(pl.BlockSpec(memory_space=pl.ANY))
        args.append(prev_out)
        aliases = {3: 0}
    pipelined = 2 * _nbytes((tq, group * hd), BF16) + 2 * _nbytes((t, hd), BF16)
    resident = 2 * _nbytes((group * tq, V7X_LANES), F32) + 3 * _nbytes((group * tq, tk), F32)
    return pl.pallas_call(
        functools.partial(_flash_bounded_kernel, group=group, head_dim=hd),
        grid=(n_seq, cfg.n_kv_heads, t // tq),
        in_specs=in_specs,
        out_specs=pl.BlockSpec((tq, group * hd), lambda b, h, i: (qb0 + b * (t // tq) + i, h)),
        out_shape=jax.ShapeDtypeStruct((n, cfg.n_q_heads * hd), BF16),
        scratch_shapes=[pltpu.VMEM((group * tq, V7X_LANES), F32), pltpu.VMEM((group * tq, V7X_LANES), F32)],
        input_output_aliases=aliases,
        compiler_params=pltpu.CompilerParams(
            dimension_semantics=("parallel", "parallel", "arbitrary"),
            vmem_limit_bytes=_vmem_limit(pipelined, resident)),
        name=f"flash_bounded_t{t}",
    )(*args)


SCORE_BOUND_LOG2 = 60.0


def attention(qkv, score_bound, cfg):
    lens = cfg.seq_lens
    runs, row0, idx = [], 0, 0
    while idx < len(lens):
        t, n_seq = lens[idx], 1
        while idx + n_seq < len(lens) and lens[idx + n_seq] == t:
            n_seq += 1
        runs.append((row0, t, n_seq))
        row0 += t * n_seq
        idx += n_seq

    def online(qkv):
        out = None
        for row0, t, n_seq in runs:
            out = _flash_call(qkv, out, row0=row0, t=t, n_seq=n_seq, cfg=cfg)
        return out

    def bounded(qkv):
        n, hd = qkv.shape[0], cfg.head_dim
        tk = _tile(math.gcd(*lens), 1024)
        k = qkv[:, cfg.n_q_heads * hd:(cfg.n_q_heads + cfg.n_kv_heads) * hd]
        kt = k.reshape(n // tk, tk, cfg.n_kv_heads, hd).transpose(2, 0, 3, 1)
        out = None
        for row0, t, n_seq in runs:
            out = _flash_bounded_call(qkv, kt, out, row0=row0, t=t, n_seq=n_seq, tk=tk, cfg=cfg)
        return out

    return lax.cond(score_bound <= SCORE_BOUND_LOG2, bounded, online, qkv)


def _log_sigmoid(x):
    return jnp.minimum(x, 0.0) - jnp.log1p(jnp.exp(-jnp.abs(x)))


def _gla_kernel(starts_ref, ends_ref, q_ref, k_ref, v_ref, a_ref, wa2_ref, ba_ref, o_ref, st_ref, g_ref,
                *, chunk, n_chunks, n_blocks, tau, q_scale):
    d = pl.program_id(1)
    i = pl.program_id(2)
    fwd = d == 0
    blk = jnp.where(fwd, i, n_blocks - 1 - i)
    boundary = jnp.where(fwd, starts_ref[blk], ends_ref[blk])

    @pl.when(boundary == 1)
    def _():
        st_ref[...] = jnp.zeros_like(st_ref)

    dk, dv = st_ref.shape
    row = lax.broadcasted_iota(jnp.int32, (chunk, chunk), 0)
    col = lax.broadcasted_iota(jnp.int32, (chunk, chunk), 1)
    mask = (row - col) * (1 - 2 * d) >= d
    row_dk = lax.broadcasted_iota(jnp.int32, (chunk, dk), 0)
    log_q_scale = math.log(q_scale)

    logits = jnp.dot(a_ref[...].astype(BF16), wa2_ref[0], preferred_element_type=F32)
    g_ref[...] = _log_sigmoid(logits + ba_ref[0]) / tau

    for c in range(n_chunks):
        cc = jnp.where(fwd, c, n_chunks - 1 - c)
        rows = pl.ds(pl.multiple_of(cc * chunk, chunk), chunk)
        q = q_ref[rows, :]
        k = k_ref[rows, :]
        v = v_ref[rows, :]
        g = g_ref[rows, :]
        prefix = g
        shift = 1
        while shift < chunk:
            prefix = prefix + jnp.where(row_dk >= shift, pltpu.roll(prefix, shift, 0), 0.0)
            shift *= 2
        b_all = prefix[chunk - 1:chunk, :]
        bcum = jnp.where(fwd, prefix, b_all - prefix + g)
        q_t = (q * jnp.exp(bcum + log_q_scale)).astype(BF16)
        k_t = (k * jnp.exp(-bcum)).astype(BF16)
        k_dec = (k * jnp.exp(b_all - bcum)).astype(BF16)
        attn = lax.dot_general(q_t, k_t, (((1,), (1,)), ((), ())), preferred_element_type=F32)
        attn = jnp.where(mask, attn, 0.0).astype(BF16)
        st = st_ref[...]
        o = jnp.dot(attn, v, preferred_element_type=F32)
        o += jnp.dot(q_t, st.astype(BF16), preferred_element_type=F32)
        o_ref[0, rows, :] = o
        dec = jnp.transpose(jnp.broadcast_to(jnp.exp(b_all), (V7X_LANES, dk)))
        dec = jnp.concatenate([dec] * (dv // V7X_LANES), axis=1)
        st_ref[...] = st * dec + lax.dot_general(
            k_dec, v, (((0,), (0,)), ((), ())), preferred_element_type=F32)


def gla_scan(qk, v, a, wa2, ba, cfg):
    n = qk.shape[0]
    nh, dk, dv, chunk = cfg.gla_heads, cfg.gla_dk, cfg.gla_dv, cfg.gla_chunk
    rblk = _tile(math.gcd(*cfg.seq_lens), 256)
    n_blocks = n // rblk
    starts, ends, row = [0] * n_blocks, [0] * n_blocks, 0
    for t in cfg.seq_lens:
        starts[row // rblk] = 1
        row += t
        ends[row // rblk - 1] = 1
    starts = jnp.asarray(starts, jnp.int32)
    ends = jnp.asarray(ends, jnp.int32)

    def rb(d, i):
        return i + d * (n_blocks - 1 - 2 * i)

    grid_spec = pltpu.PrefetchScalarGridSpec(
        num_scalar_prefetch=2,
        grid=(nh, 2, n_blocks),
        in_specs=[
            pl.BlockSpec((rblk, dk), lambda h, d, i, s, e: (rb(d, i), h)),
            pl.BlockSpec((rblk, dk), lambda h, d, i, s, e: (rb(d, i), nh + h)),
            pl.BlockSpec((rblk, dv), lambda h, d, i, s, e: (rb(d, i), h)),
            pl.BlockSpec((rblk, V7X_LANES), lambda h, d, i, s, e: (rb(d, i), 0)),
            pl.BlockSpec((1, V7X_LANES, dk), lambda h, d, i, s, e: (d, 0, h)),
            pl.BlockSpec((1, 1, dk), lambda h, d, i, s, e: (d, 0, h)),
        ],
        out_specs=pl.BlockSpec((1, rblk, dv), lambda h, d, i, s, e: (d, rb(d, i), h)),
        scratch_shapes=[pltpu.VMEM((dk, dv), F32), pltpu.VMEM((rblk, dk), F32)],
    )
    pipelined = (2 * _nbytes((rblk, dk), F32) + _nbytes((rblk, dv), BF16) + _nbytes((rblk, V7X_LANES), F32)
                 + _nbytes((V7X_LANES, dk), BF16) + _nbytes((rblk, dv), F32))
    return pl.pallas_call(
        functools.partial(_gla_kernel, chunk=chunk, n_chunks=rblk // chunk, n_blocks=n_blocks,
                          tau=cfg.gla_tau, q_scale=dk ** -0.5),
        grid_spec=grid_spec,
        out_shape=jax.ShapeDtypeStruct((2, n, nh * dv), F32),
        compiler_params=pltpu.CompilerParams(
            dimension_semantics=("parallel", "arbitrary", "arbitrary"),
            vmem_limit_bytes=_vmem_limit(pipelined, 4 * _nbytes((dv, dk), F32))),
        name="gla_scan",
    )(starts, ends, qk, qk, v, a, wa2, ba)


def _trunk(x, p, cfg):
    d = cfg.d_model
    hd = cfg.head_dim
    n = x.shape[0]
    bf = lambda w: w.astype(BF16)
    TM = 1024

    h = rmsnorm(x, p["norm_mix"][0], BF16, cfg)
    w_qkv = bf(jnp.concatenate([p["attn_wq"][0], p["attn_wk"][0], p["attn_wv"][0]], axis=1))
    cos, sin_lo, sin_hi = _rope_tables(cfg)
    tn_qkv = math.gcd(4 * hd, cfg.n_kv_heads * hd)
    tab = lambda arr: (arr, (TM if n >= TM else n, hd), lambda i, j: (i, 0))
    vec = lambda arr: (arr.reshape(1, -1), (1, arr.size), lambda i, j: (0, 0))
    q_scale = hd ** -0.5 * math.log2(math.e)
    qkv = matmul(
        h, [w_qkv],
        functools.partial(_epi_qkv, n_q_blocks=cfg.n_q_heads * hd // tn_qkv,
                          n_k_blocks=cfg.n_kv_heads * hd // tn_qkv, head_dim=hd, eps=cfg.eps),
        out_cols=w_qkv.shape[1], out_dtype=BF16, tm=TM, tn=tn_qkv, tk=d,
        extras=[tab(cos), tab(sin_lo), tab(sin_hi),
                vec(p["attn_q_norm"][0] * q_scale), vec(p["attn_k_norm"][0])],
        name="qkv_proj")
    score_bound = (hd * jnp.max(jnp.abs(p["attn_q_norm"][0] * q_scale)) * jnp.max(jnp.abs(p["attn_k_norm"][0]))
                   * (1.0 + 2.0 ** -6))
    o = attention(qkv, score_bound, cfg)
    res = lambda arr, tn: (arr, (TM if n >= TM else n, tn), lambda i, j: (i, j))
    x = matmul(o, [bf(p["attn_wo"][0])], _epi_residual, out_cols=d, out_dtype=F32,
               tm=TM, tn=1024, tk=o.shape[1], extras=[res(x, 1024)], name="attn_out")

    h = rmsnorm(x, p["norm_ffn"][0], BF16, cfg)
    hid = matmul(h, [bf(p["ffn_w1"][0]), bf(p["ffn_w3"][0])], _epi_swiglu,
                 out_cols=cfg.d_ff, out_dtype=BF16, tm=TM, tn=512, tk=d, name="ffn_up")
    x = matmul(hid, [bf(p["ffn_w2"][0])], _epi_residual, out_cols=d, out_dtype=F32,
               tm=TM, tn=1024, tk=2048, extras=[res(x, 1024)], name="ffn_down")

    nh, dk, dv, rank = cfg.gla_heads, cfg.gla_dk, cfg.gla_dv, cfg.gla_rank
    h = rmsnorm(x, p["norm_mix"][1], BF16, cfg)
    qk = matmul(h, [bf(jnp.concatenate([p["gla_wq"][0], p["gla_wk"][0]], axis=1))], _epi_store,
                out_cols=2 * nh * dk, out_dtype=F32, tm=TM, tn=1024, tk=d, name="gla_qk")
    v = matmul(h, [bf(p["gla_wv"][0])], _epi_store, out_cols=nh * dv, out_dtype=BF16,
               tm=TM, tn=1024, tk=d, name="gla_v")
    wa1 = jnp.zeros((d, V7X_LANES), F32)
    wa1 = wa1.at[:, :rank].set(p["gla_wa1_f"][0]).at[:, rank:2 * rank].set(p["gla_wa1_b"][0])
    a = matmul(h, [bf(wa1)], _epi_store, out_cols=V7X_LANES, out_dtype=F32,
               tm=TM, tn=V7X_LANES, tk=d, name="gla_gate_lowrank")
    wa2 = jnp.zeros((2, V7X_LANES, nh * dk), F32)
    wa2 = wa2.at[0, :rank].set(p["gla_wa2_f"][0]).at[1, rank:2 * rank].set(p["gla_wa2_b"][0])
    ba = jnp.stack([p["gla_ba_f"][0], p["gla_ba_b"][0]]).reshape(2, 1, nh * dk)
    o2 = gla_scan(qk, v, a, bf(wa2), ba, cfg)
    gated = matmul(
        h, [bf(p["gla_wg"][0])], functools.partial(_epi_gla_gate, eps=cfg.eps),
        out_cols=nh * dv, out_dtype=BF16, tm=TM // 2, tn=dv, tk=d,
        extras=[(o2, (2, TM // 2 if n >= TM // 2 else n, dv), lambda i, j: (0, i, j)),
                (jnp.tile(p["gla_o_norm"][0], nh).reshape(1, nh * dv), (1, dv), lambda i, j: (0, j))],
        name="gla_gate")
    x = matmul(gated, [bf(p["gla_wo"][0])], _epi_residual, out_cols=d, out_dtype=F32,
               tm=TM, tn=1024, tk=nh * dv, extras=[res(x, 1024)], name="gla_out")

    ne, eff = cfg.n_experts, cfg.moe_d_ff
    h, gates = rmsnorm_router(x, p["norm_ffn"][1], p["moe_router"][0], cfg)
    w1 = bf(p["moe_w1"][0]).transpose(1, 0, 2).reshape(d, ne * eff)
    w3 = bf(p["moe_w3"][0]).transpose(1, 0, 2).reshape(d, ne * eff)
    w2 = bf(p["moe_w2"][0]).reshape(ne * eff, d)
    tn_moe = min(512, eff)
    hid = matmul(h, [w1, w3], functools.partial(_epi_swiglu_gated, cols_per_expert=eff, tn=tn_moe),
                 out_cols=ne * eff, out_dtype=BF16, tm=TM, tn=tn_moe, tk=d,
                 extras=[(gates, (TM if n >= TM else n, V7X_LANES), lambda i, j: (i, 0))],
                 name="moe_up")
    x = matmul(hid, [w2], _epi_residual, out_cols=d, out_dtype=F32,
               tm=TM, tn=1024, tk=2048, extras=[res(x, 1024)], name="moe_down")

    return rmsnorm(x, p["norm_final"], F32, cfg)


def kernel(x_prompt, x_sample, norm_mix, norm_ffn, norm_final, attn_wq, attn_wk, attn_wv, attn_q_norm, attn_k_norm, attn_wo, gla_wq, gla_wk, gla_wv, gla_wg, gla_wa1_f, gla_wa2_f, gla_ba_f, gla_wa1_b, gla_wa2_b, gla_ba_b, gla_o_norm, gla_wo, ffn_w1, ffn_w3, ffn_w2, moe_router, moe_w1, moe_w3, moe_w2):
    cfg = PROD
    d = cfg.d_model
    params = dict(
        norm_mix=norm_mix, norm_ffn=norm_ffn, norm_final=norm_final,
        attn_wq=attn_wq, attn_wk=attn_wk, attn_wv=attn_wv, attn_q_norm=attn_q_norm,
        attn_k_norm=attn_k_norm, attn_wo=attn_wo,
        gla_wq=gla_wq, gla_wk=gla_wk, gla_wv=gla_wv, gla_wg=gla_wg,
        gla_wa1_f=gla_wa1_f, gla_wa2_f=gla_wa2_f, gla_ba_f=gla_ba_f,
        gla_wa1_b=gla_wa1_b, gla_wa2_b=gla_wa2_b, gla_ba_b=gla_ba_b,
        gla_o_norm=gla_o_norm, gla_wo=gla_wo,
        ffn_w1=ffn_w1, ffn_w3=ffn_w3, ffn_w2=ffn_w2,
        moe_router=moe_router, moe_w1=moe_w1, moe_w3=moe_w3, moe_w2=moe_w2)
    n_prompt = x_prompt.shape[0] * x_prompt.shape[1]
    x = jnp.concatenate([x_prompt.reshape(-1, d), x_sample.reshape(-1, d)], axis=0)
    y = _trunk(x, params, cfg)
    return (y[:n_prompt].reshape(x_prompt.shape), y[n_prompt:].reshape(x_sample.shape))
```

```python
import functools
import math
from typing import NamedTuple

import jax
import jax.numpy as jnp
from jax import lax
from jax.experimental import pallas as pl
from jax.experimental.pallas import tpu as pltpu

F32 = jnp.float32
BF16 = jnp.bfloat16

V7X_VMEM_BYTES = 64 * 1024 * 1024
V7X_LANES = 128
VMEM_CAP_BYTES = V7X_VMEM_BYTES - 8 * 1024 * 1024


class Cfg(NamedTuple):
    d_model: int
    seq_lens: tuple
    grid_w: int
    head_dim: int
    n_q_heads: int
    n_kv_heads: int
    rope_theta: float
    gla_heads: int
    gla_dk: int
    gla_dv: int
    gla_rank: int
    gla_tau: float
    gla_chunk: int
    d_ff: int
    n_experts: int
    moe_d_ff: int
    eps: float


PROD = Cfg(
    d_model=4096, seq_lens=(16384, 2048, 2048, 2048, 2048), grid_w=64,
    head_dim=128, n_q_heads=32, n_kv_heads=8, rope_theta=10000.0,
    gla_heads=4, gla_dk=512, gla_dv=1024, gla_rank=16, gla_tau=16.0, gla_chunk=64,
    d_ff=8192, n_experts=8, moe_d_ff=1024, eps=1e-6)


def _vmem_limit(pipelined_bytes, resident_bytes):
    need = 2 * pipelined_bytes + resident_bytes
    return int(min(VMEM_CAP_BYTES, max(need, 16 * 1024 * 1024)))


def _nbytes(shape, dtype):
    return math.prod(shape) * jnp.dtype(dtype).itemsize


def _tile(n, pref):
    t = min(n, pref)
    assert n % t == 0, (n, pref)
    return t


def _rms(x, eps):
    return x * lax.rsqrt(jnp.mean(x * x, axis=-1, keepdims=True) + eps)


def _rmsnorm_kernel(x_ref, w_ref, o_ref, *, eps):
    o_ref[...] = (_rms(x_ref[...], eps) * w_ref[...]).astype(o_ref.dtype)


def rmsnorm(x, w, out_dtype, cfg):
    n, d = x.shape
    tm = _tile(n, 256)
    return pl.pallas_call(
        functools.partial(_rmsnorm_kernel, eps=cfg.eps),
        grid=(n // tm,),
        in_specs=[pl.BlockSpec((tm, d), lambda i: (i, 0)),
                  pl.BlockSpec((1, d), lambda i: (0, 0))],
        out_specs=pl.BlockSpec((tm, d), lambda i: (i, 0)),
        out_shape=jax.ShapeDtypeStruct((n, d), out_dtype),
        compiler_params=pltpu.CompilerParams(
            dimension_semantics=("parallel",),
            vmem_limit_bytes=_vmem_limit(_nbytes((tm, d), F32) + _nbytes((tm, d), out_dtype),
                                         2 * _nbytes((tm, d), F32))),
        name="rmsnorm",
    )(x, w.reshape(1, d))


def _router_kernel(x_ref, w_ref, r_ref, route_ref, *, eps, n_experts):
    h = _rms(x_ref[...], eps) * w_ref[...]
    logits = jnp.dot(h, r_ref[...], precision=lax.Precision.HIGHEST, preferred_element_type=F32)
    lane = lax.broadcasted_iota(jnp.int32, logits.shape, 1)
    neg = jnp.float32(-jnp.inf)
    logits = jnp.where(lane < n_experts, logits, neg)
    v1 = jnp.max(logits, axis=-1, keepdims=True)
    i1 = jnp.min(jnp.where(logits == v1, lane, V7X_LANES), axis=-1, keepdims=True)
    rest = jnp.where(lane == i1, neg, logits)
    v2 = jnp.max(rest, axis=-1, keepdims=True)
    i2 = jnp.min(jnp.where(rest == v2, lane, V7X_LANES), axis=-1, keepdims=True)
    e2 = jnp.exp(v2 - v1)
    denom = 1.0 + e2
    route = jnp.where(lane == 0, i1.astype(F32), 0.0) + jnp.where(lane == 1, i2.astype(F32), 0.0)
    route_ref[...] = route + jnp.where(lane == 2, 1.0 / denom, 0.0) + jnp.where(lane == 3, e2 / denom, 0.0)


def router(x, w, router_w, cfg):
    n, d = x.shape
    tm = _tile(n, 256)
    r_pad = jnp.zeros((d, V7X_LANES), F32).at[:, :cfg.n_experts].set(router_w)
    return pl.pallas_call(
        functools.partial(_router_kernel, eps=cfg.eps, n_experts=cfg.n_experts),
        grid=(n // tm,),
        in_specs=[pl.BlockSpec((tm, d), lambda i: (i, 0)),
                  pl.BlockSpec((1, d), lambda i: (0, 0)),
                  pl.BlockSpec((d, V7X_LANES), lambda i: (0, 0))],
        out_specs=pl.BlockSpec((tm, V7X_LANES), lambda i: (i, 0)),
        out_shape=jax.ShapeDtypeStruct((n, V7X_LANES), F32),
        compiler_params=pltpu.CompilerParams(
            dimension_semantics=("parallel",),
            vmem_limit_bytes=_vmem_limit(_nbytes((tm, d), F32) + _nbytes((d, V7X_LANES), F32),
                                         4 * _nbytes((tm, d), F32))),
        name="router",
    )(x, w.reshape(1, d), r_pad)


def _row_copy(src_hbm, dst_vmem, sem, src_row, dst_row):
    return pltpu.make_async_copy(src_hbm.at[pl.ds(src_row, 1), :], dst_vmem.at[pl.ds(dst_row, 1), :], sem)


def _moe_up_kernel(tile_expert_ref, row_token_ref, n_used_ref, x_hbm, nw_ref, rw_ref, w1_ref, w3_ref, hid_ref,
                   xbuf, xn_ref, sem, *, tm, eps):
    i = pl.program_id(0)
    j = pl.program_id(1)
    used = i < n_used_ref[0]

    @pl.when(jnp.logical_and(used, j == 0))
    def _():
        base = i * tm

        def start(r, c):
            _row_copy(x_hbm, xbuf, sem, row_token_ref[base + r], r).start()
            return c

        lax.fori_loop(0, tm, start, 0, unroll=8)

        def wait(r, c):
            _row_copy(x_hbm, xbuf, sem, 0, r).wait()
            return c

        lax.fori_loop(0, tm, wait, 0, unroll=8)
        xn_ref[...] = (_rms(xbuf[...], eps) * nw_ref[...]).astype(xn_ref.dtype)

    @pl.when(used)
    def _():
        xn = xn_ref[...]
        a1 = jnp.dot(xn, w1_ref[0], preferred_element_type=F32)
        a3 = jnp.dot(xn, w3_ref[0], preferred_element_type=F32)
        hid_ref[...] = (_silu(a1) * a3 * rw_ref[...]).astype(hid_ref.dtype)

    @pl.when(jnp.logical_not(used))
    def _():
        hid_ref[...] = jnp.zeros_like(hid_ref)


def _moe_down_kernel(tile_expert_ref, n_used_ref, hid_ref, w2_ref, y_ref):
    @pl.when(pl.program_id(0) < n_used_ref[0])
    def _():
        y_ref[...] = jnp.dot(hid_ref[...], w2_ref[0], preferred_element_type=F32)

    @pl.when(pl.program_id(0) >= n_used_ref[0])
    def _():
        y_ref[...] = jnp.zeros_like(y_ref)


def _moe_combine_kernel(pos_ref, x_ref, nw_ref, y_hbm, o_ref, ybuf, sem, *, tb, n_tokens, top_k, eps):
    base = pl.program_id(0) * tb

    def start(r, c):
        for s in range(top_k):
            _row_copy(y_hbm, ybuf.at[s], sem, pos_ref[s * n_tokens + base + r], r).start()
        return c

    lax.fori_loop(0, tb, start, 0, unroll=8)

    def wait(r, c):
        for s in range(top_k):
            _row_copy(y_hbm, ybuf.at[s], sem, 0, r).wait()
        return c

    lax.fori_loop(0, tb, wait, 0, unroll=8)
    acc = x_ref[...]
    for s in range(top_k):
        acc = acc + ybuf[s]
    o_ref[...] = (_rms(acc, eps) * nw_ref[...]).astype(o_ref.dtype)


def _route_plan(route, tm, cfg):
    n, ne, top_k = route.shape[0], cfg.n_experts, 2
    expert = route[:, :top_k].astype(jnp.int32).T.reshape(-1)
    weight = route[:, top_k:2 * top_k].T.reshape(-1)
    token = jnp.tile(jnp.arange(n, dtype=jnp.int32), top_k)
    p_rows = top_k * n + ne * tm
    n_tiles = p_rows // tm
    counts = jnp.sum(expert[:, None] == jnp.arange(ne)[None, :], axis=0).astype(jnp.int32)
    padded = ((counts + tm - 1) // tm) * tm
    pad_end = jnp.cumsum(padded)
    pad_start = pad_end - padded
    raw_start = jnp.cumsum(counts) - counts
    order = jnp.argsort(expert, stable=True)
    sorted_expert = expert[order]
    dest_sorted = pad_start[sorted_expert] + (jnp.arange(top_k * n, dtype=jnp.int32) - raw_start[sorted_expert])
    pos = jnp.zeros((top_k * n,), jnp.int32).at[order].set(dest_sorted)
    row_token = jnp.zeros((p_rows,), jnp.int32).at[pos].set(token)
    row_weight = jnp.zeros((p_rows,), F32).at[pos].set(weight).reshape(p_rows, 1)
    tile_start = jnp.arange(n_tiles, dtype=jnp.int32) * tm
    tile_expert = jnp.minimum(jnp.sum(tile_start[:, None] >= pad_end[None, :], axis=1), ne - 1).astype(jnp.int32)
    n_used = (pad_end[-1:] // tm).astype(jnp.int32)
    return tile_expert, row_token, row_weight, n_used, pos


def routed_moe_final(x, route, norm_w, w1, w3, w2, final_w, cfg):
    n, d = x.shape
    ne, eff, top_k = cfg.n_experts, cfg.moe_d_ff, 2
    tm = _tile(n, 512)
    tile_expert, row_token, row_weight, n_used, pos = _route_plan(route, tm, cfg)
    p_rows = row_token.shape[0]
    n_tiles = p_rows // tm
    tn = _tile(eff, 512)

    hid = pl.pallas_call(
        functools.partial(_moe_up_kernel, tm=tm, eps=cfg.eps),
        grid_spec=pltpu.PrefetchScalarGridSpec(
            num_scalar_prefetch=3,
            grid=(n_tiles, eff // tn),
            in_specs=[
                pl.BlockSpec(memory_space=pl.ANY),
                pl.BlockSpec((1, d), lambda i, j, te, rt, nu: (0, 0)),
                pl.BlockSpec((tm, 1), lambda i, j, te, rt, nu: (i, 0)),
                pl.BlockSpec((1, d, tn), lambda i, j, te, rt, nu: (te[i], 0, j)),
                pl.BlockSpec((1, d, tn), lambda i, j, te, rt, nu: (te[i], 0, j)),
            ],
            out_specs=pl.BlockSpec((tm, tn), lambda i, j, te, rt, nu: (i, j)),
            scratch_shapes=[pltpu.VMEM((tm, d), F32), pltpu.VMEM((tm, d), BF16), pltpu.SemaphoreType.DMA(())],
        ),
        out_shape=jax.ShapeDtypeStruct((p_rows, eff), BF16),
        compiler_params=pltpu.CompilerParams(
            dimension_semantics=("arbitrary", "arbitrary"),
            vmem_limit_bytes=_vmem_limit(
                2 * _nbytes((d, tn), BF16) + _nbytes((tm, tn), BF16) + _nbytes((tm, V7X_LANES), F32),
                2 * _nbytes((tm, d), F32) + _nbytes((tm, d), BF16) + 4 * _nbytes((tm, tn), F32))),
        name="moe_up",
    )(tile_expert, row_token, n_used, x, norm_w.reshape(1, d), row_weight, w1, w3)

    tn2 = _tile(d, 2048)
    y = pl.pallas_call(
        _moe_down_kernel,
        grid_spec=pltpu.PrefetchScalarGridSpec(
            num_scalar_prefetch=2,
            grid=(n_tiles, d // tn2),
            in_specs=[
                pl.BlockSpec((tm, eff), lambda i, j, te, nu: (i, 0)),
                pl.BlockSpec((1, eff, tn2), lambda i, j, te, nu: (te[i], 0, j)),
            ],
            out_specs=pl.BlockSpec((tm, tn2), lambda i, j, te, nu: (i, j)),
        ),
        out_shape=jax.ShapeDtypeStruct((p_rows, d), F32),
        compiler_params=pltpu.CompilerParams(
            dimension_semantics=("parallel", "parallel"),
            vmem_limit_bytes=_vmem_limit(
                _nbytes((tm, eff), BF16) + _nbytes((eff, tn2), BF16) + _nbytes((tm, tn2), F32),
                2 * _nbytes((tm, tn2), F32))),
        name="moe_down",
    )(tile_expert, n_used, hid, w2)

    tb = _tile(n, 256)
    return pl.pallas_call(
        functools.partial(_moe_combine_kernel, tb=tb, n_tokens=n, top_k=top_k, eps=cfg.eps),
        grid_spec=pltpu.PrefetchScalarGridSpec(
            num_scalar_prefetch=1,
            grid=(n // tb,),
            in_specs=[
                pl.BlockSpec((tb, d), lambda i, ps: (i, 0)),
                pl.BlockSpec((1, d), lambda i, ps: (0, 0)),
                pl.BlockSpec(memory_space=pl.ANY),
            ],
            out_specs=pl.BlockSpec((tb, d), lambda i, ps: (i, 0)),
            scratch_shapes=[pltpu.VMEM((top_k, tb, d), F32), pltpu.SemaphoreType.DMA(())],
        ),
        out_shape=jax.ShapeDtypeStruct((n, d), F32),
        compiler_params=pltpu.CompilerParams(
            dimension_semantics=("arbitrary",),
            vmem_limit_bytes=_vmem_limit(2 * _nbytes((tb, d), F32), (top_k + 2) * _nbytes((tb, d), F32))),
        name="moe_combine_final_norm",
    )(pos, x, final_w.reshape(1, d), y)


def _mm_kernel(*refs, n_w, n_extra, nk, epilogue):
    a_ref = refs[0]
    w_refs = refs[1:1 + n_w]
    extra_refs = refs[1 + n_w:1 + n_w + n_extra]
    o_ref = refs[1 + n_w + n_extra]
    acc_refs = refs[2 + n_w + n_extra:]
    a = a_ref[...]
    parts = [jnp.dot(a, w[...], preferred_element_type=F32) for w in w_refs]
    if nk == 1:
        epilogue(parts, extra_refs, o_ref)
        return
    k = pl.program_id(2)

    @pl.when(k == 0)
    def _():
        for acc, p in zip(acc_refs, parts):
            acc[...] = p

    @pl.when(jnp.logical_and(k > 0, k < nk - 1))
    def _():
        for acc, p in zip(acc_refs, parts):
            acc[...] += p

    @pl.when(k == nk - 1)
    def _():
        epilogue([acc[...] + p for acc, p in zip(acc_refs, parts)], extra_refs, o_ref)


def matmul(a, ws, epilogue, *, out_cols, out_dtype, tm, tn, tk, extras=(), name):
    m, kdim = a.shape
    n = ws[0].shape[1]
    tm, tn, tk = _tile(m, tm), _tile(n, tn), _tile(kdim, tk)
    nk = kdim // tk
    assert out_cols == n
    in_specs = [pl.BlockSpec((tm, tk), lambda i, j, k: (i, k))]
    in_specs += [pl.BlockSpec((tk, tn), lambda i, j, k: (k, j)) for _ in ws]
    pipelined = _nbytes((tm, tk), a.dtype) + len(ws) * _nbytes((tk, tn), ws[0].dtype)
    for arr, blk, imap in extras:
        in_specs.append(pl.BlockSpec(blk, lambda i, j, k, imap=imap: imap(i, j)))
        pipelined += _nbytes(blk, arr.dtype)
    pipelined += _nbytes((tm, tn), out_dtype)
    acc_bytes = len(ws) * _nbytes((tm, tn), F32)
    scratch = [pltpu.VMEM((tm, tn), F32) for _ in ws] if nk > 1 else []
    return pl.pallas_call(
        functools.partial(_mm_kernel, n_w=len(ws), n_extra=len(extras), nk=nk, epilogue=epilogue),
        grid=(m // tm, n // tn, nk),
        in_specs=in_specs,
        out_specs=pl.BlockSpec((tm, tn), lambda i, j, k: (i, j)),
        out_shape=jax.ShapeDtypeStruct((m, out_cols), out_dtype),
        scratch_shapes=scratch,
        compiler_params=pltpu.CompilerParams(
            dimension_semantics=("parallel", "parallel", "arbitrary"),
            vmem_limit_bytes=_vmem_limit(pipelined, 3 * acc_bytes)),
        name=name,
    )(a, *ws, *[e[0] for e in extras])


def _epi_store(parts, extra_refs, o_ref):
    o_ref[...] = parts[0].astype(o_ref.dtype)


def _epi_residual(parts, extra_refs, o_ref):
    o_ref[...] = (extra_refs[0][...] + parts[0]).astype(o_ref.dtype)


def _silu(x):
    return x * (1.0 / (1.0 + jnp.exp(-x)))


def _epi_swiglu(parts, extra_refs, o_ref):
    o_ref[...] = (_silu(parts[0]) * parts[1]).astype(o_ref.dtype)


def _epi_qkv(parts, extra_refs, o_ref, *, n_q_blocks, n_k_blocks, head_dim, eps):
    acc = parts[0]
    cos_ref, sin_lo_ref, sin_hi_ref, qw_ref, kw_ref = extra_refs
    j = pl.program_id(1)
    heads = acc.shape[1] // head_dim

    def norm_rope(w):
        cos, sin_lo, sin_hi = cos_ref[...], sin_lo_ref[...], sin_hi_ref[...]
        for g in range(heads):
            sl = slice(g * head_dim, (g + 1) * head_dim)
            y = _rms(acc[:, sl], eps) * w
            r = (y * cos + pltpu.roll(y, head_dim - head_dim // 4, 1) * sin_lo
                 + pltpu.roll(y, head_dim // 4, 1) * sin_hi)
            o_ref[:, sl] = r.astype(o_ref.dtype)

    @pl.when(j < n_q_blocks)
    def _():
        norm_rope(qw_ref[...])

    @pl.when(jnp.logical_and(j >= n_q_blocks, j < n_q_blocks + n_k_blocks))
    def _():
        norm_rope(kw_ref[...])

    @pl.when(j >= n_q_blocks + n_k_blocks)
    def _():
        o_ref[...] = acc.astype(o_ref.dtype)


def _epi_gla_gate(parts, extra_refs, o_ref, *, eps):
    o2_ref, w_ref = extra_refs
    o = o2_ref[0] + o2_ref[1]
    o_ref[...] = (_rms(o, eps) * w_ref[...] * _silu(parts[0])).astype(o_ref.dtype)


def _rope_tables(cfg):
    hd = cfg.head_dim
    axis_dim = hd // 2
    inv_freq = cfg.rope_theta ** (-jnp.arange(0, axis_dim, 2, dtype=F32) / axis_dim)
    tabs = []
    for t in cfg.seq_lens:
        pos = jnp.arange(t)
        ang_r = (pos // cfg.grid_w).astype(F32)[:, None] * inv_freq
        ang_c = (pos % cfg.grid_w).astype(F32)[:, None] * inv_freq
        tabs.append(jnp.concatenate([ang_r, ang_r, ang_c, ang_c], axis=-1))
    ang = jnp.concatenate(tabs, axis=0)
    cos, sin = jnp.cos(ang), jnp.sin(ang)
    first_half = (jnp.arange(hd) % (hd // 2)) < (hd // 4)
    sin_lo = jnp.where(first_half, -sin, 0.0)
    sin_hi = jnp.where(first_half, 0.0, sin)
    return cos, sin_lo, sin_hi


def _flash_kernel(q_ref, k_ref, v_ref, *rest, tk, group, head_dim):
    o_ref = rest[-1]
    t = k_ref.shape[0]
    tq = q_ref.shape[0]
    qs = [q_ref[:, g * head_dim:(g + 1) * head_dim] for g in range(group)]

    def body(kb, carry):
        off = pl.multiple_of(kb * tk, tk)
        kblk = k_ref[pl.ds(off, tk), :]
        vblk = v_ref[pl.ds(off, tk), :]
        out = []
        for g in range(group):
            m_prev, l_prev, acc_prev = carry[g]
            s = lax.dot_general(qs[g], kblk, (((1,), (1,)), ((), ())), preferred_element_type=F32)
            m_new = jnp.maximum(m_prev, jnp.max(s, axis=-1, keepdims=True))
            p = jnp.exp2(s - m_new)
            alpha = jnp.exp2(m_prev - m_new)
            l_new = alpha * l_prev + jnp.sum(p, axis=-1, keepdims=True)
            acc_new = alpha * acc_prev + jnp.dot(p.astype(BF16), vblk, preferred_element_type=F32)
            out.append((m_new, l_new, acc_new))
        return tuple(out)

    init = tuple((jnp.full((tq, 1), -jnp.inf, F32), jnp.zeros((tq, 1), F32),
                  jnp.zeros((tq, head_dim), F32)) for _ in range(group))
    final = lax.fori_loop(0, t // tk, body, init)
    for g in range(group):
        _, l, acc = final[g]
        o_ref[:, g * head_dim:(g + 1) * head_dim] = (acc * (1.0 / l)).astype(o_ref.dtype)


def _flash_call(qkv, prev_out, *, row0, t, n_seq, cfg):
    n = qkv.shape[0]
    hd, group = cfg.head_dim, cfg.n_q_heads // cfg.n_kv_heads
    tq = _tile(t, 256)
    tk = _tile(t, 512)
    assert row0 % t == 0 and row0 % tq == 0
    qb0, sb0 = row0 // tq, row0 // t
    k_col0 = cfg.n_q_heads
    v_col0 = cfg.n_q_heads + cfg.n_kv_heads
    in_specs = [
        pl.BlockSpec((tq, group * hd), lambda b, h, i: (qb0 + b * (t // tq) + i, h)),
        pl.BlockSpec((t, hd), lambda b, h, i: (sb0 + b, k_col0 + h)),
        pl.BlockSpec((t, hd), lambda b, h, i: (sb0 + b, v_col0 + h)),
    ]
    args = [qkv, qkv, qkv]
    aliases = {}
    if prev_out is not None:
        in_specs.append(pl.BlockSpec(memory_space=pl.ANY))
        args.append(prev_out)
        aliases = {3: 0}
    pipelined = 2 * _nbytes((tq, group * hd), BF16) + 2 * _nbytes((t, hd), BF16)
    temps = 6 * group * _nbytes((tq, tk), F32)
    return pl.pallas_call(
        functools.partial(_flash_kernel, tk=tk, group=group, head_dim=hd),
        grid=(n_seq, cfg.n_kv_heads, t // tq),
        in_specs=in_specs,
        out_specs=pl.BlockSpec((tq, group * hd), lambda b, h, i: (qb0 + b * (t // tq) + i, h)),
        out_shape=jax.ShapeDtypeStruct((n, cfg.n_q_heads * hd), BF16),
        input_output_aliases=aliases,
        compiler_params=pltpu.CompilerParams(
            dimension_semantics=("parallel", "parallel", "arbitrary"),
            vmem_limit_bytes=_vmem_limit(pipelined, temps)),
        name=f"flash_t{t}",
    )(*args)


def _flash_bounded_kernel(qt_ref, k_ref, vt_ref, *rest, group, head_dim):
    o_ref, l_ref, acc_ref = rest[-3:]
    tq = qt_ref.shape[1]
    n_kb, tk = vt_ref.shape[1], vt_ref.shape[3]
    m = group * tq
    qt = jnp.concatenate([qt_ref[g * head_dim:(g + 1) * head_dim, :] for g in range(group)], axis=1)
    l_ref[...] = jnp.zeros_like(l_ref)
    acc_ref[...] = jnp.zeros_like(acc_ref)

    def body(kb, carry):
        kblk = k_ref[pl.ds(pl.multiple_of(kb * tk, tk), tk), :]
        pt = jnp.exp2(jnp.dot(kblk, qt, preferred_element_type=F32))
        l_ref[...] += jnp.sum(pt.reshape(tk // 8, 8, m), axis=0)
        acc_ref[...] += jnp.dot(vt_ref[0, kb], pt.astype(BF16), preferred_element_type=F32)
        return carry

    lax.fori_loop(0, n_kb, body, 0, unroll=min(4, n_kb))
    ot = acc_ref[...] * (1.0 / jnp.sum(l_ref[...], axis=0, keepdims=True))
    for g in range(group):
        o_ref[:, g * head_dim:(g + 1) * head_dim] = jnp.transpose(ot[:, g * tq:(g + 1) * tq]).astype(o_ref.dtype)


def _flash_bounded_call(qkv, qt, vt, prev_out, *, row0, t, n_seq, tk, cfg):
    n = qkv.shape[0]
    hd, group = cfg.head_dim, cfg.n_q_heads // cfg.n_kv_heads
    tq = _tile(t, 256)
    assert row0 % t == 0 and row0 % tq == 0 and t % tk == 0
    qb0, sb0 = row0 // tq, row0 // t
    k_col0 = cfg.n_q_heads
    in_specs = [
        pl.BlockSpec((group * hd, tq), lambda b, h, i: (h, qb0 + b * (t // tq) + i)),
        pl.BlockSpec((t, hd), lambda b, h, i: (sb0 + b, k_col0 + h)),
        pl.BlockSpec((1, t // tk, hd, tk), lambda b, h, i: (h, sb0 + b, 0, 0)),
    ]
    args = [qt, qkv, vt]
    aliases = {}
    if prev_out is not None:
        in_specs.append(pl.BlockSpec(memory_space=pl.ANY))
        args.append(prev_out)
        aliases = {3: 0}
    m = group * tq
    pipelined = 2 * _nbytes((tq, group * hd), BF16) + 2 * _nbytes((t, hd), BF16)
    resident = _nbytes((8 + hd, m), F32) + 3 * _nbytes((tk, m), F32)
    return pl.pallas_call(
        functools.partial(_flash_bounded_kernel, group=group, head_dim=hd),
        grid=(n_seq, cfg.n_kv_heads, t // tq),
        in_specs=in_specs,
        out_specs=pl.BlockSpec((tq, group * hd), lambda b, h, i: (qb0 + b * (t // tq) + i, h)),
        out_shape=jax.ShapeDtypeStruct((n, cfg.n_q_heads * hd), BF16),
        scratch_shapes=[pltpu.VMEM((8, m), F32), pltpu.VMEM((hd, m), F32)],
        input_output_aliases=aliases,
        compiler_params=pltpu.CompilerParams(
            dimension_semantics=("parallel", "parallel", "arbitrary"),
            vmem_limit_bytes=_vmem_limit(pipelined, resident)),
        name=f"flash_bounded_t{t}",
    )(*args)


SCORE_BOUND_LOG2 = 60.0


def attention(qkv, score_bound, cfg):
    lens = cfg.seq_lens
    runs, row0, idx = [], 0, 0
    while idx < len(lens):
        t, n_seq = lens[idx], 1
        while idx + n_seq < len(lens) and lens[idx + n_seq] == t:
            n_seq += 1
        runs.append((row0, t, n_seq))
        row0 += t * n_seq
        idx += n_seq

    def out_init(qkv):
        if len(runs) == 1:
            return None
        return jnp.zeros((qkv.shape[0], cfg.n_q_heads * cfg.head_dim), BF16)

    def online(qkv):
        out = out_init(qkv)
        for row0, t, n_seq in runs:
            out = _flash_call(qkv, out, row0=row0, t=t, n_seq=n_seq, cfg=cfg)
        return out

    def bounded(qkv):
        n, hd = qkv.shape[0], cfg.head_dim
        tk = _tile(math.gcd(*lens), 1024)
        nq, nkv = cfg.n_q_heads, cfg.n_kv_heads
        qt = qkv[:, :nq * hd].T
        vt = qkv[:, (nq + nkv) * hd:].reshape(n // tk, tk, nkv, hd).transpose(2, 0, 3, 1)
        out = out_init(qkv)
        for row0, t, n_seq in runs:
            out = _flash_bounded_call(qkv, qt, vt, out, row0=row0, t=t, n_seq=n_seq, tk=tk, cfg=cfg)
        return out

    return lax.cond(score_bound <= SCORE_BOUND_LOG2, bounded, online, qkv)


def _log_sigmoid(x):
    return jnp.minimum(x, 0.0) - jnp.log1p(jnp.exp(-jnp.abs(x)))


def _gla_kernel(starts_ref, ends_ref, q_ref, k_ref, v_ref, a_ref, wa2_ref, ba_ref, o_ref, st_ref, g_ref,
                *, chunk, n_chunks, n_blocks, tau, q_scale):
    d = pl.program_id(1)
    i = pl.program_id(2)
    fwd = d == 0
    blk = jnp.where(fwd, i, n_blocks - 1 - i)
    boundary = jnp.where(fwd, starts_ref[blk], ends_ref[blk])

    @pl.when(boundary == 1)
    def _():
        st_ref[...] = jnp.zeros_like(st_ref)

    dk, dv = st_ref.shape
    row = lax.broadcasted_iota(jnp.int32, (chunk, chunk), 0)
    col = lax.broadcasted_iota(jnp.int32, (chunk, chunk), 1)
    mask = (row - col) * (1 - 2 * d) >= d
    row_dk = lax.broadcasted_iota(jnp.int32, (chunk, dk), 0)
    log_q_scale = math.log(q_scale)

    logits = jnp.dot(a_ref[...].astype(BF16), wa2_ref[0], preferred_element_type=F32)
    g_ref[...] = _log_sigmoid(logits + ba_ref[0]) / tau

    for c in range(n_chunks):
        cc = jnp.where(fwd, c, n_chunks - 1 - c)
        rows = pl.ds(pl.multiple_of(cc * chunk, chunk), chunk)
        q = q_ref[rows, :]
        k = k_ref[rows, :]
        v = v_ref[rows, :]
        g = g_ref[rows, :]
        prefix = g
        shift = 1
        while shift < chunk:
            prefix = prefix + jnp.where(row_dk >= shift, pltpu.roll(prefix, shift, 0), 0.0)
            shift *= 2
        b_all = prefix[chunk - 1:chunk, :]
        bcum = jnp.where(fwd, prefix, b_all - prefix + g)
        q_t = (q * jnp.exp(bcum + log_q_scale)).astype(BF16)
        k_t = (k * jnp.exp(-bcum)).astype(BF16)
        k_dec = (k * jnp.exp(b_all - bcum)).astype(BF16)
        attn = lax.dot_general(q_t, k_t, (((1,), (1,)), ((), ())), preferred_element_type=F32)
        attn = jnp.where(mask, attn, 0.0).astype(BF16)
        st = st_ref[...]
        o = jnp.dot(attn, v, preferred_element_type=F32)
        o += jnp.dot(q_t, st.astype(BF16), preferred_element_type=F32)
        o_ref[0, rows, :] = o
        dec = jnp.transpose(jnp.broadcast_to(jnp.exp(b_all), (V7X_LANES, dk)))
        dec = jnp.concatenate([dec] * (dv // V7X_LANES), axis=1)
        st_ref[...] = st * dec + lax.dot_general(
            k_dec, v, (((0,), (0,)), ((), ())), preferred_element_type=F32)


def gla_scan(qk, v, a, wa2, ba, cfg):
    n = qk.shape[0]
    nh, dk, dv, chunk = cfg.gla_heads, cfg.gla_dk, cfg.gla_dv, cfg.gla_chunk
    rblk = _tile(math.gcd(*cfg.seq_lens), 256)
    n_blocks = n // rblk
    starts, ends, row = [0] * n_blocks, [0] * n_blocks, 0
    for t in cfg.seq_lens:
        starts[row // rblk] = 1
        row += t
        ends[row // rblk - 1] = 1
    starts = jnp.asarray(starts, jnp.int32)
    ends = jnp.asarray(ends, jnp.int32)

    def rb(d, i):
        return i + d * (n_blocks - 1 - 2 * i)

    grid_spec = pltpu.PrefetchScalarGridSpec(
        num_scalar_prefetch=2,
        grid=(nh, 2, n_blocks),
        in_specs=[
            pl.BlockSpec((rblk, dk), lambda h, d, i, s, e: (rb(d, i), h)),
            pl.BlockSpec((rblk, dk), lambda h, d, i, s, e: (rb(d, i), nh + h)),
            pl.BlockSpec((rblk, dv), lambda h, d, i, s, e: (rb(d, i), h)),
            pl.BlockSpec((rblk, V7X_LANES), lambda h, d, i, s, e: (rb(d, i), 0)),
            pl.BlockSpec((1, V7X_LANES, dk), lambda h, d, i, s, e: (d, 0, h)),
            pl.BlockSpec((1, 1, dk), lambda h, d, i, s, e: (d, 0, h)),
        ],
        out_specs=pl.BlockSpec((1, rblk, dv), lambda h, d, i, s, e: (d, rb(d, i), h)),
        scratch_shapes=[pltpu.VMEM((dk, dv), F32), pltpu.VMEM((rblk, dk), F32)],
    )
    pipelined = (2 * _nbytes((rblk, dk), F32) + _nbytes((rblk, dv), BF16) + _nbytes((rblk, V7X_LANES), F32)
                 + _nbytes((V7X_LANES, dk), BF16) + _nbytes((rblk, dv), F32))
    return pl.pallas_call(
        functools.partial(_gla_kernel, chunk=chunk, n_chunks=rblk // chunk, n_blocks=n_blocks,
                          tau=cfg.gla_tau, q_scale=dk ** -0.5),
        grid_spec=grid_spec,
        out_shape=jax.ShapeDtypeStruct((2, n, nh * dv), F32),
        compiler_params=pltpu.CompilerParams(
            dimension_semantics=("parallel", "arbitrary", "arbitrary"),
            vmem_limit_bytes=_vmem_limit(pipelined, 4 * _nbytes((dv, dk), F32))),
        name="gla_scan",
    )(starts, ends, qk, qk, v, a, wa2, ba)


def _trunk(x, p, cfg):
    d = cfg.d_model
    hd = cfg.head_dim
    n = x.shape[0]
    bf = lambda w: w.astype(BF16)
    TM = 1024

    h = rmsnorm(x, p["norm_mix"][0], BF16, cfg)
    w_qkv = bf(jnp.concatenate([p["attn_wq"][0], p["attn_wk"][0], p["attn_wv"][0]], axis=1))
    cos, sin_lo, sin_hi = _rope_tables(cfg)
    tn_qkv = math.gcd(4 * hd, cfg.n_kv_heads * hd)
    tab = lambda arr: (arr, (TM if n >= TM else n, hd), lambda i, j: (i, 0))
    vec = lambda arr: (arr.reshape(1, -1), (1, arr.size), lambda i, j: (0, 0))
    q_scale = hd ** -0.5 * math.log2(math.e)
    qkv = matmul(
        h, [w_qkv],
        functools.partial(_epi_qkv, n_q_blocks=cfg.n_q_heads * hd // tn_qkv,
                          n_k_blocks=cfg.n_kv_heads * hd // tn_qkv, head_dim=hd, eps=cfg.eps),
        out_cols=w_qkv.shape[1], out_dtype=BF16, tm=TM, tn=tn_qkv, tk=d,
        extras=[tab(cos), tab(sin_lo), tab(sin_hi),
                vec(p["attn_q_norm"][0] * q_scale), vec(p["attn_k_norm"][0])],
        name="qkv_proj")
    score_bound = (hd * jnp.max(jnp.abs(p["attn_q_norm"][0] * q_scale)) * jnp.max(jnp.abs(p["attn_k_norm"][0]))
                   * (1.0 + 2.0 ** -6))
    o = attention(qkv, score_bound, cfg)
    res = lambda arr, tn: (arr, (TM if n >= TM else n, tn), lambda i, j: (i, j))
    x = matmul(o, [bf(p["attn_wo"][0])], _epi_residual, out_cols=d, out_dtype=F32,
               tm=TM, tn=1024, tk=o.shape[1], extras=[res(x, 1024)], name="attn_out")

    h = rmsnorm(x, p["norm_ffn"][0], BF16, cfg)
    hid = matmul(h, [bf(p["ffn_w1"][0]), bf(p["ffn_w3"][0])], _epi_swiglu,
                 out_cols=cfg.d_ff, out_dtype=BF16, tm=TM, tn=512, tk=d, name="ffn_up")
    x = matmul(hid, [bf(p["ffn_w2"][0])], _epi_residual, out_cols=d, out_dtype=F32,
               tm=TM, tn=1024, tk=2048, extras=[res(x, 1024)], name="ffn_down")

    nh, dk, dv, rank = cfg.gla_heads, cfg.gla_dk, cfg.gla_dv, cfg.gla_rank
    h = rmsnorm(x, p["norm_mix"][1], BF16, cfg)
    qk = matmul(h, [bf(jnp.concatenate([p["gla_wq"][0], p["gla_wk"][0]], axis=1))], _epi_store,
                out_cols=2 * nh * dk, out_dtype=F32, tm=TM, tn=1024, tk=d, name="gla_qk")
    v = matmul(h, [bf(p["gla_wv"][0])], _epi_store, out_cols=nh * dv, out_dtype=BF16,
               tm=TM, tn=1024, tk=d, name="gla_v")
    wa1 = jnp.zeros((d, V7X_LANES), F32)
    wa1 = wa1.at[:, :rank].set(p["gla_wa1_f"][0]).at[:, rank:2 * rank].set(p["gla_wa1_b"][0])
    a = matmul(h, [bf(wa1)], _epi_store, out_cols=V7X_LANES, out_dtype=F32,
               tm=TM, tn=V7X_LANES, tk=d, name="gla_gate_lowrank")
    wa2 = jnp.zeros((2, V7X_LANES, nh * dk), F32)
    wa2 = wa2.at[0, :rank].set(p["gla_wa2_f"][0]).at[1, rank:2 * rank].set(p["gla_wa2_b"][0])
    ba = jnp.stack([p["gla_ba_f"][0], p["gla_ba_b"][0]]).reshape(2, 1, nh * dk)
    o2 = gla_scan(qk, v, a, bf(wa2), ba, cfg)
    gated = matmul(
        h, [bf(p["gla_wg"][0])], functools.partial(_epi_gla_gate, eps=cfg.eps),
        out_cols=nh * dv, out_dtype=BF16, tm=TM // 2, tn=dv, tk=d,
        extras=[(o2, (2, TM // 2 if n >= TM // 2 else n, dv), lambda i, j: (0, i, j)),
                (jnp.tile(p["gla_o_norm"][0], nh).reshape(1, nh * dv), (1, dv), lambda i, j: (0, j))],
        name="gla_gate")
    x = matmul(gated, [bf(p["gla_wo"][0])], _epi_residual, out_cols=d, out_dtype=F32,
               tm=TM, tn=1024, tk=nh * dv, extras=[res(x, 1024)], name="gla_out")

    route = router(x, p["norm_ffn"][1], p["moe_router"][0], cfg)
    return routed_moe_final(x, route, p["norm_ffn"][1], bf(p["moe_w1"][0]), bf(p["moe_w3"][0]),
                            bf(p["moe_w2"][0]), p["norm_final"], cfg)


def kernel(x_prompt, x_sample, norm_mix, norm_ffn, norm_final, attn_wq, attn_wk, attn_wv, attn_q_norm, attn_k_norm, attn_wo, gla_wq, gla_wk, gla_wv, gla_wg, gla_wa1_f, gla_wa2_f, gla_ba_f, gla_wa1_b, gla_wa2_b, gla_ba_b, gla_o_norm, gla_wo, ffn_w1, ffn_w3, ffn_w2, moe_router, moe_w1, moe_w3, moe_w2):
    cfg = PROD
    d = cfg.d_model
    params = dict(
        norm_mix=norm_mix, norm_ffn=norm_ffn, norm_final=norm_final,
        attn_wq=attn_wq, attn_wk=attn_wk, attn_wv=attn_wv, attn_q_norm=attn_q_norm,
        attn_k_norm=attn_k_norm, attn_wo=attn_wo,
        gla_wq=gla_wq, gla_wk=gla_wk, gla_wv=gla_wv, gla_wg=gla_wg,
        gla_wa1_f=gla_wa1_f, gla_wa2_f=gla_wa2_f, gla_ba_f=gla_ba_f,
        gla_wa1_b=gla_wa1_b, gla_wa2_b=gla_wa2_b, gla_ba_b=gla_ba_b,
        gla_o_norm=gla_o_norm, gla_wo=gla_wo,
        ffn_w1=ffn_w1, ffn_w3=ffn_w3, ffn_w2=ffn_w2,
        moe_router=moe_router, moe_w1=moe_w1, moe_w3=moe_w3, moe_w2=moe_w2)
    n_prompt = x_prompt.shape[0] * x_prompt.shape[1]
    x = jnp.concatenate([x_prompt.reshape(-1, d), x_sample.reshape(-1, d)], axis=0)
    y = _trunk(x, params, cfg)
    return (y[:n_prompt].reshape(x_prompt.shape), y[n_prompt:].reshape(x_sample.shape))
```

```python
import functools
import math
from typing import NamedTuple

import jax
import jax.numpy as jnp
from jax import lax
from jax.experimental import pallas as pl
from jax.experimental.pallas import tpu as pltpu

F32 = jnp.float32
BF16 = jnp.bfloat16

V7X_VMEM_BYTES = 64 * 1024 * 1024
V7X_LANES = 128
VMEM_CAP_BYTES = V7X_VMEM_BYTES - 8 * 1024 * 1024


class Cfg(NamedTuple):
    d_model: int
    seq_lens: tuple
    grid_w: int
    head_dim: int
    n_q_heads: int
    n_kv_heads: int
    rope_theta: float
    gla_heads: int
    gla_dk: int
    gla_dv: int
    gla_rank: int
    gla_tau: float
    gla_chunk: int
    d_ff: int
    n_experts: int
    moe_d_ff: int
    eps: float


PROD = Cfg(
    d_model=4096, seq_lens=(16384, 2048, 2048, 2048, 2048), grid_w=64,
    head_dim=128, n_q_heads=32, n_kv_heads=8, rope_theta=10000.0,
    gla_heads=4, gla_dk=512, gla_dv=1024, gla_rank=16, gla_tau=16.0, gla_chunk=64,
    d_ff=8192, n_experts=8, moe_d_ff=1024, eps=1e-6)


def _vmem_limit(pipelined_bytes, resident_bytes):
    need = 2 * pipelined_bytes + resident_bytes
    return int(min(VMEM_CAP_BYTES, max(need, 16 * 1024 * 1024)))


def _nbytes(shape, dtype):
    return math.prod(shape) * jnp.dtype(dtype).itemsize


def _tile(n, pref):
    t = min(n, pref)
    assert n % t == 0, (n, pref)
    return t


def _rms(x, eps):
    return x * lax.rsqrt(jnp.mean(x * x, axis=-1, keepdims=True) + eps)


def _rmsnorm_kernel(x_ref, w_ref, o_ref, *, eps):
    o_ref[...] = (_rms(x_ref[...], eps) * w_ref[...]).astype(o_ref.dtype)


def rmsnorm(x, w, out_dtype, cfg):
    n, d = x.shape
    tm = _tile(n, 256)
    return pl.pallas_call(
        functools.partial(_rmsnorm_kernel, eps=cfg.eps),
        grid=(n // tm,),
        in_specs=[pl.BlockSpec((tm, d), lambda i: (i, 0)),
                  pl.BlockSpec((1, d), lambda i: (0, 0))],
        out_specs=pl.BlockSpec((tm, d), lambda i: (i, 0)),
        out_shape=jax.ShapeDtypeStruct((n, d), out_dtype),
        compiler_params=pltpu.CompilerParams(
            dimension_semantics=("parallel",),
            vmem_limit_bytes=_vmem_limit(_nbytes((tm, d), F32) + _nbytes((tm, d), out_dtype),
                                         2 * _nbytes((tm, d), F32))),
        name="rmsnorm",
    )(x, w.reshape(1, d))


def _rmsnorm_pair_kernel(xa_ref, xb_ref, w_ref, o_ref, *, eps, a_blocks):
    i = pl.program_id(0)

    @pl.when(i < a_blocks)
    def _():
        o_ref[...] = (_rms(xa_ref[...], eps) * w_ref[...]).astype(o_ref.dtype)

    @pl.when(i >= a_blocks)
    def _():
        o_ref[...] = (_rms(xb_ref[...], eps) * w_ref[...]).astype(o_ref.dtype)


def rmsnorm_pair(xa, xb, w, out_dtype, cfg):
    na, nb, d = xa.shape[0], xb.shape[0], xa.shape[1]
    tm = _tile(math.gcd(na, nb), 256)
    a_blocks = na // tm
    return pl.pallas_call(
        functools.partial(_rmsnorm_pair_kernel, eps=cfg.eps, a_blocks=a_blocks),
        grid=((na + nb) // tm,),
        in_specs=[pl.BlockSpec((tm, d), lambda i: (jnp.minimum(i, a_blocks - 1), 0)),
                  pl.BlockSpec((tm, d), lambda i: (jnp.maximum(i - a_blocks, 0), 0)),
                  pl.BlockSpec((1, d), lambda i: (0, 0))],
        out_specs=pl.BlockSpec((tm, d), lambda i: (i, 0)),
        out_shape=jax.ShapeDtypeStruct((na + nb, d), out_dtype),
        compiler_params=pltpu.CompilerParams(
            dimension_semantics=("arbitrary",),
            vmem_limit_bytes=_vmem_limit(2 * _nbytes((tm, d), F32) + _nbytes((tm, d), out_dtype),
                                         2 * _nbytes((tm, d), F32))),
        name="rmsnorm_pair",
    )(xa, xb, w.reshape(1, d))


def _router_kernel(x_ref, w_ref, r_ref, route_ref, *, eps, n_experts):
    h = _rms(x_ref[...], eps) * w_ref[...]
    logits = jnp.dot(h, r_ref[...], precision=lax.Precision.HIGHEST, preferred_element_type=F32)
    lane = lax.broadcasted_iota(jnp.int32, logits.shape, 1)
    neg = jnp.float32(-jnp.inf)
    logits = jnp.where(lane < n_experts, logits, neg)
    v1 = jnp.max(logits, axis=-1, keepdims=True)
    i1 = jnp.min(jnp.where(logits == v1, lane, V7X_LANES), axis=-1, keepdims=True)
    rest = jnp.where(lane == i1, neg, logits)
    v2 = jnp.max(rest, axis=-1, keepdims=True)
    i2 = jnp.min(jnp.where(rest == v2, lane, V7X_LANES), axis=-1, keepdims=True)
    e2 = jnp.exp(v2 - v1)
    denom = 1.0 + e2
    route = jnp.where(lane == 0, i1.astype(F32), 0.0) + jnp.where(lane == 1, i2.astype(F32), 0.0)
    route_ref[...] = route + jnp.where(lane == 2, 1.0 / denom, 0.0) + jnp.where(lane == 3, e2 / denom, 0.0)


def router(x, w, router_w, cfg):
    n, d = x.shape
    tm = _tile(n, 256)
    r_pad = jnp.zeros((d, V7X_LANES), F32).at[:, :cfg.n_experts].set(router_w)
    return pl.pallas_call(
        functools.partial(_router_kernel, eps=cfg.eps, n_experts=cfg.n_experts),
        grid=(n // tm,),
        in_specs=[pl.BlockSpec((tm, d), lambda i: (i, 0)),
                  pl.BlockSpec((1, d), lambda i: (0, 0)),
                  pl.BlockSpec((d, V7X_LANES), lambda i: (0, 0))],
        out_specs=pl.BlockSpec((tm, V7X_LANES), lambda i: (i, 0)),
        out_shape=jax.ShapeDtypeStruct((n, V7X_LANES), F32),
        compiler_params=pltpu.CompilerParams(
            dimension_semantics=("parallel",),
            vmem_limit_bytes=_vmem_limit(_nbytes((tm, d), F32) + _nbytes((d, V7X_LANES), F32),
                                         4 * _nbytes((tm, d), F32))),
        name="router",
    )(x, w.reshape(1, d), r_pad)


def _row_copy(src_hbm, dst_vmem, sem, src_row, dst_row):
    return pltpu.make_async_copy(src_hbm.at[pl.ds(src_row, 1), :], dst_vmem.at[pl.ds(dst_row, 1), :], sem)


def _gather_rows(src_hbm, dst_vmem, sem, row_of, n_rows):
    def start(r, c):
        _row_copy(src_hbm, dst_vmem, sem, row_of(r), r).start()
        return c

    lax.fori_loop(0, n_rows, start, 0, unroll=8)


def _wait_rows(src_hbm, dst_vmem, sem, n_rows):
    def wait(r, c):
        _row_copy(src_hbm, dst_vmem, sem, 0, r).wait()
        return c

    lax.fori_loop(0, n_rows, wait, 0, unroll=8)


def _moe_up_kernel(tile_expert_ref, tile_src_ref, tok_ref, x_hbm, nw_ref, w1_ref, w3_ref, hid_ref,
                   xbuf, xn_ref, sem, *, tm, n_tiles, eps):
    i = pl.program_id(0)
    j = pl.program_id(1)
    cur = lax.rem(i, 2)
    nxt_tile = jnp.minimum(i + 1, n_tiles - 1)

    def gather(tile, buf):
        src0 = tile_src_ref[tile]
        _gather_rows(x_hbm, xbuf.at[buf], sem.at[buf], lambda r: tok_ref[src0 + r], tm)

    def normalise(buf):
        xn_ref[buf] = (_rms(xbuf[buf], eps) * nw_ref[...]).astype(xn_ref.dtype)

    @pl.when(jnp.logical_and(i == 0, j == 0))
    def _():
        gather(0, 0)
        _wait_rows(x_hbm, xbuf.at[0], sem.at[0], tm)
        normalise(0)

    def up_half():
        xn = xn_ref[cur]
        a1 = jnp.dot(xn, w1_ref[0], preferred_element_type=F32)
        a3 = jnp.dot(xn, w3_ref[0], preferred_element_type=F32)
        hid_ref[...] = (_silu(a1) * a3).astype(hid_ref.dtype)

    @pl.when(j == 0)
    def _():
        gather(nxt_tile, 1 - cur)
        up_half()

    @pl.when(j == 1)
    def _():
        _wait_rows(x_hbm, xbuf.at[1 - cur], sem.at[1 - cur], tm)
        up_half()
        normalise(1 - cur)


def _moe_down_kernel(tile_expert_ref, hid_ref, w2_ref, y_ref):
    y_ref[...] = jnp.dot(hid_ref[...], w2_ref[0], preferred_element_type=F32)


def _moe_combine_kernel(pos_ref, x_ref, route_ref, nw_ref, y_hbm, oa_ref, ob_ref, ybuf, sem,
                        *, tb, n_tokens, n_steps, a_steps, top_k, eps):
    i = pl.program_id(0)
    cur = lax.rem(i, 2)

    def gather(step, buf):
        base = step * tb
        for s in range(top_k):
            _gather_rows(y_hbm, ybuf.at[buf, s], sem.at[buf], lambda r, s=s: pos_ref[s * n_tokens + base + r], tb)

    @pl.when(i == 0)
    def _():
        gather(0, 0)

    @pl.when(i + 1 < n_steps)
    def _():
        gather(i + 1, 1 - cur)

    for s in range(top_k):
        _wait_rows(y_hbm, ybuf.at[cur, s], sem.at[cur], tb)
    acc = x_ref[...]
    route = route_ref[...]
    for s in range(top_k):
        acc = acc + ybuf[cur, s] * route[:, top_k + s:top_k + s + 1]
    out = _rms(acc, eps) * nw_ref[...]

    @pl.when(i < a_steps)
    def _():
        oa_ref[...] = out

    @pl.when(i >= a_steps)
    def _():
        ob_ref[...] = out


def _route_plan(route, tm, cfg):
    n, ne, top_k = route.shape[0], cfg.n_experts, 2
    pairs = top_k * n
    expert = route[:, :top_k].astype(jnp.int32).T.reshape(-1)
    token = jnp.tile(jnp.arange(n, dtype=jnp.int32), top_k)
    index = jnp.arange(pairs, dtype=jnp.int32)
    sorted_expert, sorted_index, sorted_token = lax.sort((expert, index, token), num_keys=1)
    onehot = sorted_expert[:, None] == jnp.arange(ne, dtype=jnp.int32)[None, :]
    counts = jnp.sum(onehot, axis=0).astype(jnp.int32)
    padded = ((counts + tm - 1) // tm) * tm
    pad_end = jnp.cumsum(padded)
    shift = (pad_end - padded) - (jnp.cumsum(counts) - counts)
    dest_sorted = index + jnp.sum(jnp.where(onehot, shift[None, :], 0), axis=1)
    _, pos = lax.sort((sorted_index, dest_sorted), num_keys=1)
    n_tiles = (pairs + ne * tm) // tm
    tile_start = jnp.arange(n_tiles, dtype=jnp.int32) * tm
    tile_expert = jnp.minimum(jnp.sum(tile_start[:, None] >= pad_end[None, :], axis=1), ne - 1).astype(jnp.int32)
    tile_shift = jnp.sum(jnp.where(tile_expert[:, None] == jnp.arange(ne)[None, :], shift[None, :], 0), axis=1)
    tile_src = jnp.clip(tile_start - tile_shift, 0, pairs - 1).astype(jnp.int32)
    tok = jnp.concatenate([sorted_token, jnp.zeros((tm,), jnp.int32)])
    return tile_expert, tile_src, tok, pos


def routed_moe_final(x, route, norm_w, w1, w3, w2, final_w, n_first, cfg):
    n, d = x.shape
    ne, eff, top_k = cfg.n_experts, cfg.moe_d_ff, 2
    tm = _tile(n, 256)
    tile_expert, tile_src, tok, pos = _route_plan(route, tm, cfg)
    n_tiles = tile_expert.shape[0]
    p_rows = n_tiles * tm
    tn = eff // 2

    hid = pl.pallas_call(
        functools.partial(_moe_up_kernel, tm=tm, n_tiles=n_tiles, eps=cfg.eps),
        grid_spec=pltpu.PrefetchScalarGridSpec(
            num_scalar_prefetch=3,
            grid=(n_tiles, 2),
            in_specs=[
                pl.BlockSpec(memory_space=pl.ANY),
                pl.BlockSpec((1, d), lambda i, j, te, ts, tk: (0, 0)),
                pl.BlockSpec((1, d, tn), lambda i, j, te, ts, tk: (te[i], 0, j)),
                pl.BlockSpec((1, d, tn), lambda i, j, te, ts, tk: (te[i], 0, j)),
            ],
            out_specs=pl.BlockSpec((tm, tn), lambda i, j, te, ts, tk: (i, j)),
            scratch_shapes=[pltpu.VMEM((2, tm, d), F32), pltpu.VMEM((2, tm, d), BF16),
                            pltpu.SemaphoreType.DMA((2,))],
        ),
        out_shape=jax.ShapeDtypeStruct((p_rows, eff), BF16),
        compiler_params=pltpu.CompilerParams(
            dimension_semantics=("arbitrary", "arbitrary"),
            vmem_limit_bytes=_vmem_limit(
                2 * _nbytes((d, tn), BF16) + _nbytes((tm, tn), BF16),
                4 * _nbytes((tm, d), F32) + 2 * _nbytes((tm, d), BF16) + 4 * _nbytes((tm, tn), F32))),
        name="moe_up",
    )(tile_expert, tile_src, tok, x, norm_w.reshape(1, d), w1, w3)

    tn2 = _tile(d, 2048)
    y = pl.pallas_call(
        _moe_down_kernel,
        grid_spec=pltpu.PrefetchScalarGridSpec(
            num_scalar_prefetch=1,
            grid=(n_tiles, d // tn2),
            in_specs=[
                pl.BlockSpec((tm, eff), lambda i, j, te: (i, 0)),
                pl.BlockSpec((1, eff, tn2), lambda i, j, te: (te[i], 0, j)),
            ],
            out_specs=pl.BlockSpec((tm, tn2), lambda i, j, te: (i, j)),
        ),
        out_shape=jax.ShapeDtypeStruct((p_rows, d), F32),
        compiler_params=pltpu.CompilerParams(
            dimension_semantics=("parallel", "parallel"),
            vmem_limit_bytes=_vmem_limit(
                _nbytes((tm, eff), BF16) + _nbytes((eff, tn2), BF16) + _nbytes((tm, tn2), F32),
                2 * _nbytes((tm, tn2), F32))),
        name="moe_down",
    )(tile_expert, hid, w2)

    tb = _tile(math.gcd(n_first, n - n_first), 256)
    n_steps, a_steps = n // tb, n_first // tb
    return pl.pallas_call(
        functools.partial(_moe_combine_kernel, tb=tb, n_tokens=n, n_steps=n_steps, a_steps=a_steps,
                          top_k=top_k, eps=cfg.eps),
        grid_spec=pltpu.PrefetchScalarGridSpec(
            num_scalar_prefetch=1,
            grid=(n_steps,),
            in_specs=[
                pl.BlockSpec((tb, d), lambda i, ps: (i, 0)),
                pl.BlockSpec((tb, V7X_LANES), lambda i, ps: (i, 0)),
                pl.BlockSpec((1, d), lambda i, ps: (0, 0)),
                pl.BlockSpec(memory_space=pl.ANY),
            ],
            out_specs=[pl.BlockSpec((tb, d), lambda i, ps: (jnp.minimum(i, a_steps - 1), 0)),
                       pl.BlockSpec((tb, d), lambda i, ps: (jnp.maximum(i - a_steps, 0), 0))],
            scratch_shapes=[pltpu.VMEM((2, top_k, tb, d), F32), pltpu.SemaphoreType.DMA((2,))],
        ),
        out_shape=[jax.ShapeDtypeStruct((n_first, d), F32), jax.ShapeDtypeStruct((n - n_first, d), F32)],
        compiler_params=pltpu.CompilerParams(
            dimension_semantics=("arbitrary",),
            vmem_limit_bytes=_vmem_limit(3 * _nbytes((tb, d), F32), (2 * top_k + 3) * _nbytes((tb, d), F32))),
        name="moe_combine_final_norm",
    )(pos, x, route, final_w.reshape(1, d), y)


def _mm_kernel(*refs, n_w, n_extra, nk, epilogue):
    a_ref = refs[0]
    w_refs = refs[1:1 + n_w]
    extra_refs = refs[1 + n_w:1 + n_w + n_extra]
    o_ref = refs[1 + n_w + n_extra]
    acc_refs = refs[2 + n_w + n_extra:]
    a = a_ref[...]
    parts = [jnp.dot(a, w[...], preferred_element_type=F32) for w in w_refs]
    if nk == 1:
        epilogue(parts, extra_refs, o_ref)
        return
    k = pl.program_id(2)

    @pl.when(k == 0)
    def _():
        for acc, p in zip(acc_refs, parts):
            acc[...] = p

    @pl.when(jnp.logical_and(k > 0, k < nk - 1))
    def _():
        for acc, p in zip(acc_refs, parts):
            acc[...] += p

    @pl.when(k == nk - 1)
    def _():
        epilogue([acc[...] + p for acc, p in zip(acc_refs, parts)], extra_refs, o_ref)


def matmul(a, ws, epilogue, *, out_cols, out_dtype, tm, tn, tk, extras=(), name):
    m, kdim = a.shape
    n = ws[0].shape[1]
    tm, tn, tk = _tile(m, tm), _tile(n, tn), _tile(kdim, tk)
    nk = kdim // tk
    assert out_cols == n
    in_specs = [pl.BlockSpec((tm, tk), lambda i, j, k: (i, k))]
    in_specs += [pl.BlockSpec((tk, tn), lambda i, j, k: (k, j)) for _ in ws]
    pipelined = _nbytes((tm, tk), a.dtype) + len(ws) * _nbytes((tk, tn), ws[0].dtype)
    for arr, blk, imap in extras:
        in_specs.append(pl.BlockSpec(blk, lambda i, j, k, imap=imap: imap(i, j)))
        pipelined += _nbytes(blk, arr.dtype)
    pipelined += _nbytes((tm, tn), out_dtype)
    acc_bytes = len(ws) * _nbytes((tm, tn), F32)
    scratch = [pltpu.VMEM((tm, tn), F32) for _ in ws] if nk > 1 else []
    return pl.pallas_call(
        functools.partial(_mm_kernel, n_w=len(ws), n_extra=len(extras), nk=nk, epilogue=epilogue),
        grid=(m // tm, n // tn, nk),
        in_specs=in_specs,
        out_specs=pl.BlockSpec((tm, tn), lambda i, j, k: (i, j)),
        out_shape=jax.ShapeDtypeStruct((m, out_cols), out_dtype),
        scratch_shapes=scratch,
        compiler_params=pltpu.CompilerParams(
            dimension_semantics=("parallel", "parallel", "arbitrary"),
            vmem_limit_bytes=_vmem_limit(pipelined, 3 * acc_bytes)),
        name=name,
    )(a, *ws, *[e[0] for e in extras])


def _epi_store(parts, extra_refs, o_ref):
    o_ref[...] = parts[0].astype(o_ref.dtype)


def _epi_residual(parts, extra_refs, o_ref):
    o_ref[...] = (extra_refs[0][...] + parts[0]).astype(o_ref.dtype)


def _epi_residual_pair(parts, extra_refs, o_ref, *, a_blocks):
    i = pl.program_id(0)

    @pl.when(i < a_blocks)
    def _():
        o_ref[...] = (extra_refs[0][...] + parts[0]).astype(o_ref.dtype)

    @pl.when(i >= a_blocks)
    def _():
        o_ref[...] = (extra_refs[1][...] + parts[0]).astype(o_ref.dtype)


def _silu(x):
    return x * (1.0 / (1.0 + jnp.exp(-x)))


def _epi_swiglu(parts, extra_refs, o_ref):
    o_ref[...] = (_silu(parts[0]) * parts[1]).astype(o_ref.dtype)


def _epi_qkv(parts, extra_refs, o_ref, *, n_q_blocks, n_k_blocks, head_dim, eps):
    acc = parts[0]
    cos_ref, sin_lo_ref, sin_hi_ref, qw_ref, kw_ref = extra_refs
    j = pl.program_id(1)
    heads = acc.shape[1] // head_dim

    def norm_rope(w):
        cos, sin_lo, sin_hi = cos_ref[...], sin_lo_ref[...], sin_hi_ref[...]
        for g in range(heads):
            sl = slice(g * head_dim, (g + 1) * head_dim)
            y = _rms(acc[:, sl], eps) * w
            r = (y * cos + pltpu.roll(y, head_dim - head_dim // 4, 1) * sin_lo
                 + pltpu.roll(y, head_dim // 4, 1) * sin_hi)
            o_ref[:, sl] = r.astype(o_ref.dtype)

    @pl.when(j < n_q_blocks)
    def _():
        norm_rope(qw_ref[...])

    @pl.when(jnp.logical_and(j >= n_q_blocks, j < n_q_blocks + n_k_blocks))
    def _():
        norm_rope(kw_ref[...])

    @pl.when(j >= n_q_blocks + n_k_blocks)
    def _():
        o_ref[...] = acc.astype(o_ref.dtype)


def _epi_gla_gate(parts, extra_refs, o_ref, *, eps):
    o2_ref, w_ref = extra_refs
    o = o2_ref[0] + o2_ref[1]
    o_ref[...] = (_rms(o, eps) * w_ref[...] * _silu(parts[0])).astype(o_ref.dtype)


def _rope_tables(cfg):
    hd = cfg.head_dim
    axis_dim = hd // 2
    inv_freq = cfg.rope_theta ** (-jnp.arange(0, axis_dim, 2, dtype=F32) / axis_dim)
    tabs = []
    for t in cfg.seq_lens:
        pos = jnp.arange(t)
        ang_r = (pos // cfg.grid_w).astype(F32)[:, None] * inv_freq
        ang_c = (pos % cfg.grid_w).astype(F32)[:, None] * inv_freq
        tabs.append(jnp.concatenate([ang_r, ang_r, ang_c, ang_c], axis=-1))
    ang = jnp.concatenate(tabs, axis=0)
    cos, sin = jnp.cos(ang), jnp.sin(ang)
    first_half = (jnp.arange(hd) % (hd // 2)) < (hd // 4)
    sin_lo = jnp.where(first_half, -sin, 0.0)
    sin_hi = jnp.where(first_half, 0.0, sin)
    return cos, sin_lo, sin_hi


def _flash_kernel(q_ref, k_ref, v_ref, *rest, tk, group, head_dim):
    o_ref = rest[-1]
    t = k_ref.shape[0]
    tq = q_ref.shape[0]
    qs = [q_ref[:, g * head_dim:(g + 1) * head_dim] for g in range(group)]

    def body(kb, carry):
        off = pl.multiple_of(kb * tk, tk)
        kblk = k_ref[pl.ds(off, tk), :]
        vblk = v_ref[pl.ds(off, tk), :]
        out = []
        for g in range(group):
            m_prev, l_prev, acc_prev = carry[g]
            s = lax.dot_general(qs[g], kblk, (((1,), (1,)), ((), ())), preferred_element_type=F32)
            m_new = jnp.maximum(m_prev, jnp.max(s, axis=-1, keepdims=True))
            p = jnp.exp2(s - m_new)
            alpha = jnp.exp2(m_prev - m_new)
            l_new = alpha * l_prev + jnp.sum(p, axis=-1, keepdims=True)
            acc_new = alpha * acc_prev + jnp.dot(p.astype(BF16), vblk, preferred_element_type=F32)
            out.append((m_new, l_new, acc_new))
        return tuple(out)

    init = tuple((jnp.full((tq, 1), -jnp.inf, F32), jnp.zeros((tq, 1), F32),
                  jnp.zeros((tq, head_dim), F32)) for _ in range(group))
    final = lax.fori_loop(0, t // tk, body, init)
    for g in range(group):
        _, l, acc = final[g]
        o_ref[:, g * head_dim:(g + 1) * head_dim] = (acc * (1.0 / l)).astype(o_ref.dtype)


def _flash_call(qkv, prev_out, *, row0, t, n_seq, cfg):
    n = qkv.shape[0]
    hd, group = cfg.head_dim, cfg.n_q_heads // cfg.n_kv_heads
    tq = _tile(t, 256)
    tk = _tile(t, 512)
    assert row0 % t == 0 and row0 % tq == 0
    qb0, sb0 = row0 // tq, row0 // t
    k_col0 = cfg.n_q_heads
    v_col0 = cfg.n_q_heads + cfg.n_kv_heads
    in_specs = [
        pl.BlockSpec((tq, group * hd), lambda b, h, i: (qb0 + b * (t // tq) + i, h)),
        pl.BlockSpec((t, hd), lambda b, h, i: (sb0 + b, k_col0 + h)),
        pl.BlockSpec((t, hd), lambda b, h, i: (sb0 + b, v_col0 + h)),
    ]
    args = [qkv, qkv, qkv]
    aliases = {}
    if prev_out is not None:
        in_specs.append(pl.BlockSpec(memory_space=pl.ANY))
        args.append(prev_out)
        aliases = {3: 0}
    pipelined = 2 * _nbytes((tq, group * hd), BF16) + 2 * _nbytes((t, hd), BF16)
    temps = 6 * group * _nbytes((tq, tk), F32)
    return pl.pallas_call(
        functools.partial(_flash_kernel, tk=tk, group=group, head_dim=hd),
        grid=(n_seq, cfg.n_kv_heads, t // tq),
        in_specs=in_specs,
        out_specs=pl.BlockSpec((tq, group * hd), lambda b, h, i: (qb0 + b * (t // tq) + i, h)),
        out_shape=jax.ShapeDtypeStruct((n, cfg.n_q_heads * hd), BF16),
        input_output_aliases=aliases,
        compiler_params=pltpu.CompilerParams(
            dimension_semantics=("parallel", "parallel", "arbitrary"),
            vmem_limit_bytes=_vmem_limit(pipelined, temps)),
        name=f"flash_t{t}",
    )(*args)


def _flash_bounded_kernel(qt_ref, k_ref, vt_ref, *rest, group, head_dim):
    o_ref, l_ref, acc_ref = rest[-3:]
    tq = qt_ref.shape[1]
    n_kb, tk = vt_ref.shape[1], vt_ref.shape[3]
    m = group * tq
    qt = jnp.concatenate([qt_ref[g * head_dim:(g + 1) * head_dim, :] for g in range(group)], axis=1)
    l_ref[...] = jnp.zeros_like(l_ref)
    acc_ref[...] = jnp.zeros_like(acc_ref)

    def body(kb, carry):
        kblk = k_ref[pl.ds(pl.multiple_of(kb * tk, tk), tk), :]
        pt = jnp.exp2(jnp.dot(kblk, qt, preferred_element_type=F32))
        l_ref[...] += jnp.sum(pt.reshape(tk // 8, 8, m), axis=0)
        acc_ref[...] += jnp.dot(vt_ref[0, kb], pt.astype(BF16), preferred_element_type=F32)
        return carry

    lax.fori_loop(0, n_kb, body, 0, unroll=min(4, n_kb))
    ot = acc_ref[...] * (1.0 / jnp.sum(l_ref[...], axis=0, keepdims=True))
    for g in range(group):
        o_ref[:, g * head_dim:(g + 1) * head_dim] = jnp.transpose(ot[:, g * tq:(g + 1) * tq]).astype(o_ref.dtype)


def _flash_bounded_call(qkv, qt, vt, prev_out, *, row0, t, n_seq, tk, cfg):
    n = qkv.shape[0]
    hd, group = cfg.head_dim, cfg.n_q_heads // cfg.n_kv_heads
    tq = _tile(t, 256)
    assert row0 % t == 0 and row0 % tq == 0 and t % tk == 0
    qb0, sb0 = row0 // tq, row0 // t
    k_col0 = cfg.n_q_heads
    in_specs = [
        pl.BlockSpec((group * hd, tq), lambda b, h, i: (h, qb0 + b * (t // tq) + i)),
        pl.BlockSpec((t, hd), lambda b, h, i: (sb0 + b, k_col0 + h)),
        pl.BlockSpec((1, t // tk, hd, tk), lambda b, h, i: (h, sb0 + b, 0, 0)),
    ]
    args = [qt, qkv, vt]
    aliases = {}
    if prev_out is not None:
        in_specs.append(pl.BlockSpec(memory_space=pl.ANY))
        args.append(prev_out)
        aliases = {3: 0}
    m = group * tq
    pipelined = 2 * _nbytes((tq, group * hd), BF16) + 2 * _nbytes((t, hd), BF16)
    resident = _nbytes((8 + hd, m), F32) + 3 * _nbytes((tk, m), F32)
    return pl.pallas_call(
        functools.partial(_flash_bounded_kernel, group=group, head_dim=hd),
        grid=(n_seq, cfg.n_kv_heads, t // tq),
        in_specs=in_specs,
        out_specs=pl.BlockSpec((tq, group * hd), lambda b, h, i: (qb0 + b * (t // tq) + i, h)),
        out_shape=jax.ShapeDtypeStruct((n, cfg.n_q_heads * hd), BF16),
        scratch_shapes=[pltpu.VMEM((8, m), F32), pltpu.VMEM((hd, m), F32)],
        input_output_aliases=aliases,
        compiler_params=pltpu.CompilerParams(
            dimension_semantics=("parallel", "parallel", "arbitrary"),
            vmem_limit_bytes=_vmem_limit(pipelined, resident)),
        name=f"flash_bounded_t{t}",
    )(*args)


SCORE_BOUND_LOG2 = 60.0


def attention(qkv, score_bound, cfg):
    lens = cfg.seq_lens
    runs, row0, idx = [], 0, 0
    while idx < len(lens):
        t, n_seq = lens[idx], 1
        while idx + n_seq < len(lens) and lens[idx + n_seq] == t:
            n_seq += 1
        runs.append((row0, t, n_seq))
        row0 += t * n_seq
        idx += n_seq

    def out_init(qkv):
        if len(runs) == 1:
            return None
        return jnp.zeros((qkv.shape[0], cfg.n_q_heads * cfg.head_dim), BF16)

    def online(qkv):
        out = out_init(qkv)
        for row0, t, n_seq in runs:
            out = _flash_call(qkv, out, row0=row0, t=t, n_seq=n_seq, cfg=cfg)
        return out

    def bounded(qkv):
        n, hd = qkv.shape[0], cfg.head_dim
        tk = _tile(math.gcd(*lens), 1024)
        nq, nkv = cfg.n_q_heads, cfg.n_kv_heads
        qt = qkv[:, :nq * hd].T
        vt = qkv[:, (nq + nkv) * hd:].reshape(n // tk, tk, nkv, hd).transpose(2, 0, 3, 1)
        out = out_init(qkv)
        for row0, t, n_seq in runs:
            out = _flash_bounded_call(qkv, qt, vt, out, row0=row0, t=t, n_seq=n_seq, tk=tk, cfg=cfg)
        return out

    return lax.cond(score_bound <= SCORE_BOUND_LOG2, bounded, online, qkv)


def _log_sigmoid(x):
    return jnp.minimum(x, 0.0) - jnp.log1p(jnp.exp(-jnp.abs(x)))


def _gla_kernel(starts_ref, ends_ref, q_ref, k_ref, v_ref, a_ref, wa2_ref, ba_ref, o_ref, st_ref, g_ref,
                *, chunk, n_chunks, n_blocks, tau, q_scale):
    d = pl.program_id(1)
    i = pl.program_id(2)
    fwd = d == 0
    blk = jnp.where(fwd, i, n_blocks - 1 - i)
    boundary = jnp.where(fwd, starts_ref[blk], ends_ref[blk])

    @pl.when(boundary == 1)
    def _():
        st_ref[...] = jnp.zeros_like(st_ref)

    dk, dv = st_ref.shape
    row = lax.broadcasted_iota(jnp.int32, (chunk, chunk), 0)
    col = lax.broadcasted_iota(jnp.int32, (chunk, chunk), 1)
    mask = (row - col) * (1 - 2 * d) >= d
    row_dk = lax.broadcasted_iota(jnp.int32, (chunk, dk), 0)
    log_q_scale = math.log(q_scale)

    logits = jnp.dot(a_ref[...].astype(BF16), wa2_ref[0], preferred_element_type=F32)
    g_ref[...] = _log_sigmoid(logits + ba_ref[0]) / tau

    for c in range(n_chunks):
        cc = jnp.where(fwd, c, n_chunks - 1 - c)
        rows = pl.ds(pl.multiple_of(cc * chunk, chunk), chunk)
        q = q_ref[rows, :]
        k = k_ref[rows, :]
        v = v_ref[rows, :]
        g = g_ref[rows, :]
        prefix = g
        shift = 1
        while shift < chunk:
            prefix = prefix + jnp.where(row_dk >= shift, pltpu.roll(prefix, shift, 0), 0.0)
            shift *= 2
        b_all = prefix[chunk - 1:chunk, :]
        bcum = jnp.where(fwd, prefix, b_all - prefix + g)
        q_t = (q * jnp.exp(bcum + log_q_scale)).astype(BF16)
        k_t = (k * jnp.exp(-bcum)).astype(BF16)
        k_dec = (k * jnp.exp(b_all - bcum)).astype(BF16)
        attn = lax.dot_general(q_t, k_t, (((1,), (1,)), ((), ())), preferred_element_type=F32)
        attn = jnp.where(mask, attn, 0.0).astype(BF16)
        st = st_ref[...]
        o = jnp.dot(attn, v, preferred_element_type=F32)
        o += jnp.dot(q_t, st.astype(BF16), preferred_element_type=F32)
        o_ref[0, rows, :] = o
        dec = jnp.transpose(jnp.broadcast_to(jnp.exp(b_all), (V7X_LANES, dk)))
        dec = jnp.concatenate([dec] * (dv // V7X_LANES), axis=1)
        st_ref[...] = st * dec + lax.dot_general(
            k_dec, v, (((0,), (0,)), ((), ())), preferred_element_type=F32)


def gla_scan(qk, v, a, wa2, ba, cfg):
    n = qk.shape[0]
    nh, dk, dv, chunk = cfg.gla_heads, cfg.gla_dk, cfg.gla_dv, cfg.gla_chunk
    rblk = _tile(math.gcd(*cfg.seq_lens), 256)
    n_blocks = n // rblk
    starts, ends, row = [0] * n_blocks, [0] * n_blocks, 0
    for t in cfg.seq_lens:
        starts[row // rblk] = 1
        row += t
        ends[row // rblk - 1] = 1
    starts = jnp.asarray(starts, jnp.int32)
    ends = jnp.asarray(ends, jnp.int32)

    def rb(d, i):
        return i + d * (n_blocks - 1 - 2 * i)

    grid_spec = pltpu.PrefetchScalarGridSpec(
        num_scalar_prefetch=2,
        grid=(nh, 2, n_blocks),
        in_specs=[
            pl.BlockSpec((rblk, dk), lambda h, d, i, s, e: (rb(d, i), h)),
            pl.BlockSpec((rblk, dk), lambda h, d, i, s, e: (rb(d, i), nh + h)),
            pl.BlockSpec((rblk, dv), lambda h, d, i, s, e: (rb(d, i), h)),
            pl.BlockSpec((rblk, V7X_LANES), lambda h, d, i, s, e: (rb(d, i), 0)),
            pl.BlockSpec((1, V7X_LANES, dk), lambda h, d, i, s, e: (d, 0, h)),
            pl.BlockSpec((1, 1, dk), lambda h, d, i, s, e: (d, 0, h)),
        ],
        out_specs=pl.BlockSpec((1, rblk, dv), lambda h, d, i, s, e: (d, rb(d, i), h)),
        scratch_shapes=[pltpu.VMEM((dk, dv), F32), pltpu.VMEM((rblk, dk), F32)],
    )
    pipelined = (2 * _nbytes((rblk, dk), F32) + _nbytes((rblk, dv), BF16) + _nbytes((rblk, V7X_LANES), F32)
                 + _nbytes((V7X_LANES, dk), BF16) + _nbytes((rblk, dv), F32))
    return pl.pallas_call(
        functools.partial(_gla_kernel, chunk=chunk, n_chunks=rblk // chunk, n_blocks=n_blocks,
                          tau=cfg.gla_tau, q_scale=dk ** -0.5),
        grid_spec=grid_spec,
        out_shape=jax.ShapeDtypeStruct((2, n, nh * dv), F32),
        compiler_params=pltpu.CompilerParams(
            dimension_semantics=("parallel", "arbitrary", "arbitrary"),
            vmem_limit_bytes=_vmem_limit(pipelined, 4 * _nbytes((dv, dk), F32))),
        name="gla_scan",
    )(starts, ends, qk, qk, v, a, wa2, ba)


def _trunk(xa, xb, p, cfg):
    d = cfg.d_model
    hd = cfg.head_dim
    na, n = xa.shape[0], xa.shape[0] + xb.shape[0]
    bf = lambda w: w.astype(BF16)
    TM = _tile(math.gcd(na, n - na), 1024)
    a_blocks = na // TM

    h = rmsnorm_pair(xa, xb, p["norm_mix"][0], BF16, cfg)
    w_qkv = bf(jnp.concatenate([p["attn_wq"][0], p["attn_wk"][0], p["attn_wv"][0]], axis=1))
    cos, sin_lo, sin_hi = _rope_tables(cfg)
    tn_qkv = math.gcd(4 * hd, cfg.n_kv_heads * hd)
    tab = lambda arr: (arr, (TM if n >= TM else n, hd), lambda i, j: (i, 0))
    vec = lambda arr: (arr.reshape(1, -1), (1, arr.size), lambda i, j: (0, 0))
    q_scale = hd ** -0.5 * math.log2(math.e)
    qkv = matmul(
        h, [w_qkv],
        functools.partial(_epi_qkv, n_q_blocks=cfg.n_q_heads * hd // tn_qkv,
                          n_k_blocks=cfg.n_kv_heads * hd // tn_qkv, head_dim=hd, eps=cfg.eps),
        out_cols=w_qkv.shape[1], out_dtype=BF16, tm=TM, tn=tn_qkv, tk=d,
        extras=[tab(cos), tab(sin_lo), tab(sin_hi),
                vec(p["attn_q_norm"][0] * q_scale), vec(p["attn_k_norm"][0])],
        name="qkv_proj")
    score_bound = (hd * jnp.max(jnp.abs(p["attn_q_norm"][0] * q_scale)) * jnp.max(jnp.abs(p["attn_k_norm"][0]))
                   * (1.0 + 2.0 ** -6))
    o = attention(qkv, score_bound, cfg)
    res = lambda arr, tn: (arr, (TM, tn), lambda i, j: (i, j))
    x = matmul(o, [bf(p["attn_wo"][0])], functools.partial(_epi_residual_pair, a_blocks=a_blocks),
               out_cols=d, out_dtype=F32, tm=TM, tn=512, tk=o.shape[1],
               extras=[(xa, (TM, 512), lambda i, j: (jnp.minimum(i, a_blocks - 1),
                                                     jnp.where(i < a_blocks, j, d // 512 - 1))),
                       (xb, (TM, 512), lambda i, j: (jnp.maximum(i - a_blocks, 0),
                                                     jnp.where(i < a_blocks, 0, j)))],
               name="attn_out")

    h = rmsnorm(x, p["norm_ffn"][0], BF16, cfg)
    hid = matmul(h, [bf(p["ffn_w1"][0]), bf(p["ffn_w3"][0])], _epi_swiglu,
                 out_cols=cfg.d_ff, out_dtype=BF16, tm=TM, tn=512, tk=d, name="ffn_up")
    x = matmul(hid, [bf(p["ffn_w2"][0])], _epi_residual, out_cols=d, out_dtype=F32,
               tm=TM, tn=1024, tk=2048, extras=[res(x, 1024)], name="ffn_down")

    nh, dk, dv, rank = cfg.gla_heads, cfg.gla_dk, cfg.gla_dv, cfg.gla_rank
    h = rmsnorm(x, p["norm_mix"][1], BF16, cfg)
    qk = matmul(h, [bf(jnp.concatenate([p["gla_wq"][0], p["gla_wk"][0]], axis=1))], _epi_store,
                out_cols=2 * nh * dk, out_dtype=F32, tm=TM, tn=1024, tk=d, name="gla_qk")
    v = matmul(h, [bf(p["gla_wv"][0])], _epi_store, out_cols=nh * dv, out_dtype=BF16,
               tm=TM, tn=1024, tk=d, name="gla_v")
    wa1 = jnp.zeros((d, V7X_LANES), F32)
    wa1 = wa1.at[:, :rank].set(p["gla_wa1_f"][0]).at[:, rank:2 * rank].set(p["gla_wa1_b"][0])
    a = matmul(h, [bf(wa1)], _epi_store, out_cols=V7X_LANES, out_dtype=F32,
               tm=TM, tn=V7X_LANES, tk=d, name="gla_gate_lowrank")
    wa2 = jnp.zeros((2, V7X_LANES, nh * dk), F32)
    wa2 = wa2.at[0, :rank].set(p["gla_wa2_f"][0]).at[1, rank:2 * rank].set(p["gla_wa2_b"][0])
    ba = jnp.stack([p["gla_ba_f"][0], p["gla_ba_b"][0]]).reshape(2, 1, nh * dk)
    o2 = gla_scan(qk, v, a, bf(wa2), ba, cfg)
    gated = matmul(
        h, [bf(p["gla_wg"][0])], functools.partial(_epi_gla_gate, eps=cfg.eps),
        out_cols=nh * dv, out_dtype=BF16, tm=TM // 2, tn=dv, tk=d,
        extras=[(o2, (2, TM // 2 if n >= TM // 2 else n, dv), lambda i, j: (0, i, j)),
                (jnp.tile(p["gla_o_norm"][0], nh).reshape(1, nh * dv), (1, dv), lambda i, j: (0, j))],
        name="gla_gate")
    x = matmul(gated, [bf(p["gla_wo"][0])], _epi_residual, out_cols=d, out_dtype=F32,
               tm=TM, tn=1024, tk=nh * dv, extras=[res(x, 1024)], name="gla_out")

    route = router(x, p["norm_ffn"][1], p["moe_router"][0], cfg)
    return routed_moe_final(x, route, p["norm_ffn"][1], bf(p["moe_w1"][0]), bf(p["moe_w3"][0]),
                            bf(p["moe_w2"][0]), p["norm_final"], na, cfg)


def kernel(x_prompt, x_sample, norm_mix, norm_ffn, norm_final, attn_wq, attn_wk, attn_wv, attn_q_norm, attn_k_norm, attn_wo, gla_wq, gla_wk, gla_wv, gla_wg, gla_wa1_f, gla_wa2_f, gla_ba_f, gla_wa1_b, gla_wa2_b, gla_ba_b, gla_o_norm, gla_wo, ffn_w1, ffn_w3, ffn_w2, moe_router, moe_w1, moe_w3, moe_w2):
    cfg = PROD
    d = cfg.d_model
    params = dict(
        norm_mix=norm_mix, norm_ffn=norm_ffn, norm_final=norm_final,
        attn_wq=attn_wq, attn_wk=attn_wk, attn_wv=attn_wv, attn_q_norm=attn_q_norm,
        attn_k_norm=attn_k_norm, attn_wo=attn_wo,
        gla_wq=gla_wq, gla_wk=gla_wk, gla_wv=gla_wv, gla_wg=gla_wg,
        gla_wa1_f=gla_wa1_f, gla_wa2_f=gla_wa2_f, gla_ba_f=gla_ba_f,
        gla_wa1_b=gla_wa1_b, gla_wa2_b=gla_wa2_b, gla_ba_b=gla_ba_b,
        gla_o_norm=gla_o_norm, gla_wo=gla_wo,
        ffn_w1=ffn_w1, ffn_w3=ffn_w3, ffn_w2=ffn_w2,
        moe_router=moe_router, moe_w1=moe_w1, moe_w3=moe_w3, moe_w2=moe_w2)
    y_prompt, y_sample = _trunk(x_prompt.reshape(-1, d), x_sample.reshape(-1, d), params, cfg)
    return (y_prompt.reshape(x_prompt.shape), y_sample.reshape(x_sample.shape))
```

```python
import functools
import math
from typing import NamedTuple

import jax
import jax.numpy as jnp
from jax import lax
from jax.experimental import pallas as pl
from jax.experimental.pallas import tpu as pltpu

F32 = jnp.float32
BF16 = jnp.bfloat16

V7X_VMEM_BYTES = 64 * 1024 * 1024
V7X_LANES = 128
VMEM_CAP_BYTES = V7X_VMEM_BYTES - 8 * 1024 * 1024


class Cfg(NamedTuple):
    d_model: int
    seq_lens: tuple
    grid_w: int
    head_dim: int
    n_q_heads: int
    n_kv_heads: int
    rope_theta: float
    gla_heads: int
    gla_dk: int
    gla_dv: int
    gla_rank: int
    gla_tau: float
    gla_chunk: int
    d_ff: int
    n_experts: int
    moe_d_ff: int
    eps: float


PROD = Cfg(
    d_model=4096, seq_lens=(16384, 2048, 2048, 2048, 2048), grid_w=64,
    head_dim=128, n_q_heads=32, n_kv_heads=8, rope_theta=10000.0,
    gla_heads=4, gla_dk=512, gla_dv=1024, gla_rank=16, gla_tau=16.0, gla_chunk=64,
    d_ff=8192, n_experts=8, moe_d_ff=1024, eps=1e-6)


def _vmem_limit(pipelined_bytes, resident_bytes):
    need = 2 * pipelined_bytes + resident_bytes
    return int(min(VMEM_CAP_BYTES, max(need, 16 * 1024 * 1024)))


def _nbytes(shape, dtype):
    return math.prod(shape) * jnp.dtype(dtype).itemsize


def _tile(n, pref):
    t = min(n, pref)
    assert n % t == 0, (n, pref)
    return t


def _rms(x, eps):
    return x * lax.rsqrt(jnp.mean(x * x, axis=-1, keepdims=True) + eps)


def _rmsnorm_kernel(x_ref, w_ref, o_ref, *, eps):
    o_ref[...] = (_rms(x_ref[...], eps) * w_ref[...]).astype(o_ref.dtype)


def rmsnorm(x, w, out_dtype, cfg):
    n, d = x.shape
    tm = _tile(n, 256)
    return pl.pallas_call(
        functools.partial(_rmsnorm_kernel, eps=cfg.eps),
        grid=(n // tm,),
        in_specs=[pl.BlockSpec((tm, d), lambda i: (i, 0)),
                  pl.BlockSpec((1, d), lambda i: (0, 0))],
        out_specs=pl.BlockSpec((tm, d), lambda i: (i, 0)),
        out_shape=jax.ShapeDtypeStruct((n, d), out_dtype),
        compiler_params=pltpu.CompilerParams(
            dimension_semantics=("parallel",),
            vmem_limit_bytes=_vmem_limit(_nbytes((tm, d), F32) + _nbytes((tm, d), out_dtype),
                                         2 * _nbytes((tm, d), F32))),
        name="rmsnorm",
    )(x, w.reshape(1, d))


def _rmsnorm_pair_kernel(xa_ref, xb_ref, w_ref, o_ref, *, eps, a_blocks):
    i = pl.program_id(0)

    @pl.when(i < a_blocks)
    def _():
        o_ref[...] = (_rms(xa_ref[...], eps) * w_ref[...]).astype(o_ref.dtype)

    @pl.when(i >= a_blocks)
    def _():
        o_ref[...] = (_rms(xb_ref[...], eps) * w_ref[...]).astype(o_ref.dtype)


def rmsnorm_pair(xa, xb, w, out_dtype, cfg):
    na, nb, d = xa.shape[0], xb.shape[0], xa.shape[1]
    tm = _tile(math.gcd(na, nb), 256)
    a_blocks = na // tm
    return pl.pallas_call(
        functools.partial(_rmsnorm_pair_kernel, eps=cfg.eps, a_blocks=a_blocks),
        grid=((na + nb) // tm,),
        in_specs=[pl.BlockSpec((tm, d), lambda i: (jnp.minimum(i, a_blocks - 1), 0)),
                  pl.BlockSpec((tm, d), lambda i: (jnp.maximum(i - a_blocks, 0), 0)),
                  pl.BlockSpec((1, d), lambda i: (0, 0))],
        out_specs=pl.BlockSpec((tm, d), lambda i: (i, 0)),
        out_shape=jax.ShapeDtypeStruct((na + nb, d), out_dtype),
        compiler_params=pltpu.CompilerParams(
            dimension_semantics=("arbitrary",),
            vmem_limit_bytes=_vmem_limit(2 * _nbytes((tm, d), F32) + _nbytes((tm, d), out_dtype),
                                         2 * _nbytes((tm, d), F32))),
        name="rmsnorm_pair",
    )(xa, xb, w.reshape(1, d))


def _router_kernel(x_ref, w_ref, r_ref, route_ref, *, eps, n_experts):
    h = _rms(x_ref[...], eps) * w_ref[...]
    logits = jnp.dot(h, r_ref[...], precision=lax.Precision.HIGHEST, preferred_element_type=F32)
    lane = lax.broadcasted_iota(jnp.int32, logits.shape, 1)
    neg = jnp.float32(-jnp.inf)
    logits = jnp.where(lane < n_experts, logits, neg)
    v1 = jnp.max(logits, axis=-1, keepdims=True)
    i1 = jnp.min(jnp.where(logits == v1, lane, V7X_LANES), axis=-1, keepdims=True)
    rest = jnp.where(lane == i1, neg, logits)
    v2 = jnp.max(rest, axis=-1, keepdims=True)
    i2 = jnp.min(jnp.where(rest == v2, lane, V7X_LANES), axis=-1, keepdims=True)
    e2 = jnp.exp(v2 - v1)
    denom = 1.0 + e2
    route = jnp.where(lane == 0, i1.astype(F32), 0.0) + jnp.where(lane == 1, i2.astype(F32), 0.0)
    route_ref[...] = route + jnp.where(lane == 2, 1.0 / denom, 0.0) + jnp.where(lane == 3, e2 / denom, 0.0)


def router(x, w, router_w, cfg):
    n, d = x.shape
    tm = _tile(n, 256)
    r_pad = jnp.zeros((d, V7X_LANES), F32).at[:, :cfg.n_experts].set(router_w)
    return pl.pallas_call(
        functools.partial(_router_kernel, eps=cfg.eps, n_experts=cfg.n_experts),
        grid=(n // tm,),
        in_specs=[pl.BlockSpec((tm, d), lambda i: (i, 0)),
                  pl.BlockSpec((1, d), lambda i: (0, 0)),
                  pl.BlockSpec((d, V7X_LANES), lambda i: (0, 0))],
        out_specs=pl.BlockSpec((tm, V7X_LANES), lambda i: (i, 0)),
        out_shape=jax.ShapeDtypeStruct((n, V7X_LANES), F32),
        compiler_params=pltpu.CompilerParams(
            dimension_semantics=("parallel",),
            vmem_limit_bytes=_vmem_limit(_nbytes((tm, d), F32) + _nbytes((d, V7X_LANES), F32),
                                         4 * _nbytes((tm, d), F32))),
        name="router",
    )(x, w.reshape(1, d), r_pad)


def _row_copy(src_hbm, dst_vmem, sem, src_row, dst_row):
    return pltpu.make_async_copy(src_hbm.at[pl.ds(src_row, 1), :], dst_vmem.at[pl.ds(dst_row, 1), :], sem)


def _gather_rows(src_hbm, dst_vmem, sem, row_of, n_rows):
    def start(r, c):
        _row_copy(src_hbm, dst_vmem, sem, row_of(r), r).start()
        return c

    lax.fori_loop(0, n_rows, start, 0, unroll=8)


def _wait_rows(src_hbm, dst_vmem, sem, n_rows):
    def wait(r, c):
        _row_copy(src_hbm, dst_vmem, sem, 0, r).wait()
        return c

    lax.fori_loop(0, n_rows, wait, 0, unroll=8)


def _moe_up_kernel(tile_expert_ref, tile_src_ref, tok_ref, x_hbm, nw_ref, w1_ref, w3_ref, hid_ref,
                   xbuf, xn_ref, sem, *, tm, n_tiles, eps):
    i = pl.program_id(0)
    cur = lax.rem(i, 2)
    nxt_tile = jnp.minimum(i + 1, n_tiles - 1)
    half = hid_ref.shape[1] // 2

    def gather(tile, buf):
        src0 = tile_src_ref[tile]
        _gather_rows(x_hbm, xbuf.at[buf], sem.at[buf], lambda r: tok_ref[src0 + r], tm)

    def normalise(buf):
        xn_ref[buf] = (_rms(xbuf[buf], eps) * nw_ref[...]).astype(xn_ref.dtype)

    @pl.when(i == 0)
    def _():
        gather(0, 0)
        _wait_rows(x_hbm, xbuf.at[0], sem.at[0], tm)
        normalise(0)

    def up_half(c):
        xn = xn_ref[cur]
        cols = slice(c * half, (c + 1) * half)
        a1 = jnp.dot(xn, w1_ref[0, :, cols], preferred_element_type=F32)
        a3 = jnp.dot(xn, w3_ref[0, :, cols], preferred_element_type=F32)
        hid_ref[:, cols] = (_silu(a1) * a3).astype(hid_ref.dtype)

    gather(nxt_tile, 1 - cur)
    up_half(0)
    _wait_rows(x_hbm, xbuf.at[1 - cur], sem.at[1 - cur], tm)
    up_half(1)
    normalise(1 - cur)


def _moe_down_kernel(tile_expert_ref, hid_ref, w2_ref, y_ref):
    y_ref[...] = jnp.dot(hid_ref[...], w2_ref[0], preferred_element_type=F32)


def _moe_combine_kernel(pos_ref, x_ref, route_ref, nw_ref, y_hbm, oa_ref, ob_ref, ybuf, sem,
                        *, tb, n_tokens, n_steps, a_steps, top_k, eps):
    i = pl.program_id(0)
    cur = lax.rem(i, 2)

    def gather(step, buf):
        base = step * tb
        for s in range(top_k):
            _gather_rows(y_hbm, ybuf.at[buf, s], sem.at[buf], lambda r, s=s: pos_ref[s * n_tokens + base + r], tb)

    @pl.when(i == 0)
    def _():
        gather(0, 0)

    @pl.when(i + 1 < n_steps)
    def _():
        gather(i + 1, 1 - cur)

    for s in range(top_k):
        _wait_rows(y_hbm, ybuf.at[cur, s], sem.at[cur], tb)
    acc = x_ref[...]
    route = route_ref[...]
    for s in range(top_k):
        acc = acc + ybuf[cur, s] * route[:, top_k + s:top_k + s + 1]
    out = _rms(acc, eps) * nw_ref[...]

    @pl.when(i < a_steps)
    def _():
        oa_ref[...] = out

    @pl.when(i >= a_steps)
    def _():
        ob_ref[...] = out


def _route_plan(route, tm, cfg):
    n, ne, top_k = route.shape[0], cfg.n_experts, 2
    pairs = top_k * n
    expert = route[:, :top_k].astype(jnp.int32).T.reshape(-1)
    token = jnp.tile(jnp.arange(n, dtype=jnp.int32), top_k)
    index = jnp.arange(pairs, dtype=jnp.int32)
    sorted_expert, sorted_index, sorted_token = lax.sort((expert, index, token), num_keys=1)
    onehot = sorted_expert[:, None] == jnp.arange(ne, dtype=jnp.int32)[None, :]
    counts = jnp.sum(onehot, axis=0).astype(jnp.int32)
    padded = ((counts + tm - 1) // tm) * tm
    pad_end = jnp.cumsum(padded)
    shift = (pad_end - padded) - (jnp.cumsum(counts) - counts)
    dest_sorted = index + jnp.sum(jnp.where(onehot, shift[None, :], 0), axis=1)
    _, pos = lax.sort((sorted_index, dest_sorted), num_keys=1)
    n_tiles = (pairs + ne * tm) // tm
    tile_start = jnp.arange(n_tiles, dtype=jnp.int32) * tm
    tile_expert = jnp.minimum(jnp.sum(tile_start[:, None] >= pad_end[None, :], axis=1), ne - 1).astype(jnp.int32)
    tile_shift = jnp.sum(jnp.where(tile_expert[:, None] == jnp.arange(ne)[None, :], shift[None, :], 0), axis=1)
    tile_src = jnp.clip(tile_start - tile_shift, 0, pairs - 1).astype(jnp.int32)
    tok = jnp.concatenate([sorted_token, jnp.zeros((tm,), jnp.int32)])
    return tile_expert, tile_src, tok, pos


def routed_moe_final(x, route, norm_w, w1, w3, w2, final_w, n_first, cfg):
    n, d = x.shape
    ne, eff, top_k = cfg.n_experts, cfg.moe_d_ff, 2
    tm = _tile(n, 256)
    tile_expert, tile_src, tok, pos = _route_plan(route, tm, cfg)
    n_tiles = tile_expert.shape[0]
    p_rows = n_tiles * tm

    hid = pl.pallas_call(
        functools.partial(_moe_up_kernel, tm=tm, n_tiles=n_tiles, eps=cfg.eps),
        grid_spec=pltpu.PrefetchScalarGridSpec(
            num_scalar_prefetch=3,
            grid=(n_tiles,),
            in_specs=[
                pl.BlockSpec(memory_space=pl.ANY),
                pl.BlockSpec((1, d), lambda i, te, ts, tk: (0, 0)),
                pl.BlockSpec((1, d, eff), lambda i, te, ts, tk: (te[i], 0, 0)),
                pl.BlockSpec((1, d, eff), lambda i, te, ts, tk: (te[i], 0, 0)),
            ],
            out_specs=pl.BlockSpec((tm, eff), lambda i, te, ts, tk: (i, 0)),
            scratch_shapes=[pltpu.VMEM((2, tm, d), F32), pltpu.VMEM((2, tm, d), BF16),
                            pltpu.SemaphoreType.DMA((2,))],
        ),
        out_shape=jax.ShapeDtypeStruct((p_rows, eff), BF16),
        compiler_params=pltpu.CompilerParams(
            dimension_semantics=("arbitrary",),
            vmem_limit_bytes=_vmem_limit(
                2 * _nbytes((d, eff), BF16) + _nbytes((tm, eff), BF16),
                3 * _nbytes((tm, d), F32) + 2 * _nbytes((tm, d), BF16) + 3 * _nbytes((tm, eff), F32))),
        name="moe_up",
    )(tile_expert, tile_src, tok, x, norm_w.reshape(1, d), w1, w3)

    tn2 = _tile(d, 2048)
    y = pl.pallas_call(
        _moe_down_kernel,
        grid_spec=pltpu.PrefetchScalarGridSpec(
            num_scalar_prefetch=1,
            grid=(d // tn2, n_tiles),
            in_specs=[
                pl.BlockSpec((tm, eff), lambda j, i, te: (i, 0)),
                pl.BlockSpec((1, eff, tn2), lambda j, i, te: (te[i], 0, j)),
            ],
            out_specs=pl.BlockSpec((tm, tn2), lambda j, i, te: (i, j)),
        ),
        out_shape=jax.ShapeDtypeStruct((p_rows, d), F32),
        compiler_params=pltpu.CompilerParams(
            dimension_semantics=("parallel", "parallel"),
            vmem_limit_bytes=_vmem_limit(
                _nbytes((tm, eff), BF16) + _nbytes((eff, tn2), BF16) + _nbytes((tm, tn2), F32),
                2 * _nbytes((tm, tn2), F32))),
        name="moe_down",
    )(tile_expert, hid, w2)

    tb = _tile(math.gcd(n_first, n - n_first), 256)
    n_steps, a_steps = n // tb, n_first // tb
    return pl.pallas_call(
        functools.partial(_moe_combine_kernel, tb=tb, n_tokens=n, n_steps=n_steps, a_steps=a_steps,
                          top_k=top_k, eps=cfg.eps),
        grid_spec=pltpu.PrefetchScalarGridSpec(
            num_scalar_prefetch=1,
            grid=(n_steps,),
            in_specs=[
                pl.BlockSpec((tb, d), lambda i, ps: (i, 0)),
                pl.BlockSpec((tb, V7X_LANES), lambda i, ps: (i, 0)),
                pl.BlockSpec((1, d), lambda i, ps: (0, 0)),
                pl.BlockSpec(memory_space=pl.ANY),
            ],
            out_specs=[pl.BlockSpec((tb, d), lambda i, ps: (jnp.minimum(i, a_steps - 1), 0)),
                       pl.BlockSpec((tb, d), lambda i, ps: (jnp.maximum(i - a_steps, 0), 0))],
            scratch_shapes=[pltpu.VMEM((2, top_k, tb, d), F32), pltpu.SemaphoreType.DMA((2,))],
        ),
        out_shape=[jax.ShapeDtypeStruct((n_first, d), F32), jax.ShapeDtypeStruct((n - n_first, d), F32)],
        compiler_params=pltpu.CompilerParams(
            dimension_semantics=("arbitrary",),
            vmem_limit_bytes=_vmem_limit(3 * _nbytes((tb, d), F32), (2 * top_k + 3) * _nbytes((tb, d), F32))),
        name="moe_combine_final_norm",
    )(pos, x, route, final_w.reshape(1, d), y)


def _mm_kernel(*refs, n_w, n_extra, nk, epilogue):
    a_ref = refs[0]
    w_refs = refs[1:1 + n_w]
    extra_refs = refs[1 + n_w:1 + n_w + n_extra]
    o_ref = refs[1 + n_w + n_extra]
    acc_refs = refs[2 + n_w + n_extra:]
    a = a_ref[...]
    parts = [jnp.dot(a, w[...], preferred_element_type=F32) for w in w_refs]
    if nk == 1:
        epilogue(parts, extra_refs, o_ref)
        return
    k = pl.program_id(2)

    @pl.when(k == 0)
    def _():
        for acc, p in zip(acc_refs, parts):
            acc[...] = p

    @pl.when(jnp.logical_and(k > 0, k < nk - 1))
    def _():
        for acc, p in zip(acc_refs, parts):
            acc[...] += p

    @pl.when(k == nk - 1)
    def _():
        epilogue([acc[...] + p for acc, p in zip(acc_refs, parts)], extra_refs, o_ref)


def matmul(a, ws, epilogue, *, out_cols, out_dtype, tm, tn, tk, extras=(), name):
    m, kdim = a.shape
    n = ws[0].shape[1]
    tm, tn, tk = _tile(m, tm), _tile(n, tn), _tile(kdim, tk)
    nk = kdim // tk
    assert out_cols == n
    in_specs = [pl.BlockSpec((tm, tk), lambda i, j, k: (i, k))]
    in_specs += [pl.BlockSpec((tk, tn), lambda i, j, k: (k, j)) for _ in ws]
    pipelined = _nbytes((tm, tk), a.dtype) + len(ws) * _nbytes((tk, tn), ws[0].dtype)
    for arr, blk, imap in extras:
        in_specs.append(pl.BlockSpec(blk, lambda i, j, k, imap=imap: imap(i, j)))
        pipelined += _nbytes(blk, arr.dtype)
    pipelined += _nbytes((tm, tn), out_dtype)
    acc_bytes = len(ws) * _nbytes((tm, tn), F32)
    scratch = [pltpu.VMEM((tm, tn), F32) for _ in ws] if nk > 1 else []
    return pl.pallas_call(
        functools.partial(_mm_kernel, n_w=len(ws), n_extra=len(extras), nk=nk, epilogue=epilogue),
        grid=(m // tm, n // tn, nk),
        in_specs=in_specs,
        out_specs=pl.BlockSpec((tm, tn), lambda i, j, k: (i, j)),
        out_shape=jax.ShapeDtypeStruct((m, out_cols), out_dtype),
        scratch_shapes=scratch,
        compiler_params=pltpu.CompilerParams(
            dimension_semantics=("parallel", "parallel", "arbitrary"),
            vmem_limit_bytes=_vmem_limit(pipelined, 3 * acc_bytes)),
        name=name,
    )(a, *ws, *[e[0] for e in extras])


def _epi_store(parts, extra_refs, o_ref):
    o_ref[...] = parts[0].astype(o_ref.dtype)


def _epi_residual(parts, extra_refs, o_ref):
    o_ref[...] = (extra_refs[0][...] + parts[0]).astype(o_ref.dtype)


def _epi_residual_pair(parts, extra_refs, o_ref, *, a_blocks):
    i = pl.program_id(0)

    @pl.when(i < a_blocks)
    def _():
        o_ref[...] = (extra_refs[0][...] + parts[0]).astype(o_ref.dtype)

    @pl.when(i >= a_blocks)
    def _():
        o_ref[...] = (extra_refs[1][...] + parts[0]).astype(o_ref.dtype)


def _silu(x):
    return x * (1.0 / (1.0 + jnp.exp(-x)))


def _epi_swiglu(parts, extra_refs, o_ref):
    o_ref[...] = (_silu(parts[0]) * parts[1]).astype(o_ref.dtype)


def _epi_qkv(parts, extra_refs, o_ref, *, n_q_blocks, n_k_blocks, head_dim, eps):
    acc = parts[0]
    cos_ref, sin_lo_ref, sin_hi_ref, qw_ref, kw_ref = extra_refs
    j = pl.program_id(1)
    heads = acc.shape[1] // head_dim

    def norm_rope(w):
        cos, sin_lo, sin_hi = cos_ref[...], sin_lo_ref[...], sin_hi_ref[...]
        for g in range(heads):
            sl = slice(g * head_dim, (g + 1) * head_dim)
            y = _rms(acc[:, sl], eps) * w
            r = (y * cos + pltpu.roll(y, head_dim - head_dim // 4, 1) * sin_lo
                 + pltpu.roll(y, head_dim // 4, 1) * sin_hi)
            o_ref[:, sl] = r.astype(o_ref.dtype)

    @pl.when(j < n_q_blocks)
    def _():
        norm_rope(qw_ref[...])

    @pl.when(jnp.logical_and(j >= n_q_blocks, j < n_q_blocks + n_k_blocks))
    def _():
        norm_rope(kw_ref[...])

    @pl.when(j >= n_q_blocks + n_k_blocks)
    def _():
        o_ref[...] = acc.astype(o_ref.dtype)


def _epi_gla_gate(parts, extra_refs, o_ref, *, eps):
    of_ref, ob_ref, w_ref = extra_refs
    o = of_ref[...] + ob_ref[...]
    o_ref[...] = (_rms(o, eps) * w_ref[...] * _silu(parts[0])).astype(o_ref.dtype)


def _rope_tables(cfg):
    hd = cfg.head_dim
    axis_dim = hd // 2
    inv_freq = cfg.rope_theta ** (-jnp.arange(0, axis_dim, 2, dtype=F32) / axis_dim)
    tabs = []
    for t in cfg.seq_lens:
        pos = jnp.arange(t)
        ang_r = (pos // cfg.grid_w).astype(F32)[:, None] * inv_freq
        ang_c = (pos % cfg.grid_w).astype(F32)[:, None] * inv_freq
        tabs.append(jnp.concatenate([ang_r, ang_r, ang_c, ang_c], axis=-1))
    ang = jnp.concatenate(tabs, axis=0)
    cos, sin = jnp.cos(ang), jnp.sin(ang)
    first_half = (jnp.arange(hd) % (hd // 2)) < (hd // 4)
    sin_lo = jnp.where(first_half, -sin, 0.0)
    sin_hi = jnp.where(first_half, 0.0, sin)
    return cos, sin_lo, sin_hi


def _flash_kernel(q_ref, k_ref, v_ref, *rest, tk, group, head_dim):
    o_ref = rest[-1]
    t = k_ref.shape[0]
    tq = q_ref.shape[0]
    qs = [q_ref[:, g * head_dim:(g + 1) * head_dim] for g in range(group)]

    def body(kb, carry):
        off = pl.multiple_of(kb * tk, tk)
        kblk = k_ref[pl.ds(off, tk), :]
        vblk = v_ref[pl.ds(off, tk), :]
        out = []
        for g in range(group):
            m_prev, l_prev, acc_prev = carry[g]
            s = lax.dot_general(qs[g], kblk, (((1,), (1,)), ((), ())), preferred_element_type=F32)
            m_new = jnp.maximum(m_prev, jnp.max(s, axis=-1, keepdims=True))
            p = jnp.exp2(s - m_new)
            alpha = jnp.exp2(m_prev - m_new)
            l_new = alpha * l_prev + jnp.sum(p, axis=-1, keepdims=True)
            acc_new = alpha * acc_prev + jnp.dot(p.astype(BF16), vblk, preferred_element_type=F32)
            out.append((m_new, l_new, acc_new))
        return tuple(out)

    init = tuple((jnp.full((tq, 1), -jnp.inf, F32), jnp.zeros((tq, 1), F32),
                  jnp.zeros((tq, head_dim), F32)) for _ in range(group))
    final = lax.fori_loop(0, t // tk, body, init)
    for g in range(group):
        _, l, acc = final[g]
        o_ref[:, g * head_dim:(g + 1) * head_dim] = (acc * (1.0 / l)).astype(o_ref.dtype)


def _flash_call(qkv, prev_out, *, row0, t, n_seq, cfg):
    n = qkv.shape[0]
    hd, group = cfg.head_dim, cfg.n_q_heads // cfg.n_kv_heads
    tq = _tile(t, 256)
    tk = _tile(t, 512)
    assert row0 % t == 0 and row0 % tq == 0
    qb0, sb0 = row0 // tq, row0 // t
    k_col0 = cfg.n_q_heads
    v_col0 = cfg.n_q_heads + cfg.n_kv_heads
    in_specs = [
        pl.BlockSpec((tq, group * hd), lambda b, h, i: (qb0 + b * (t // tq) + i, h)),
        pl.BlockSpec((t, hd), lambda b, h, i: (sb0 + b, k_col0 + h)),
        pl.BlockSpec((t, hd), lambda b, h, i: (sb0 + b, v_col0 + h)),
    ]
    args = [qkv, qkv, qkv]
    aliases = {}
    if prev_out is not None:
        in_specs.append(pl.BlockSpec(memory_space=pl.ANY))
        args.append(prev_out)
        aliases = {3: 0}
    pipelined = 2 * _nbytes((tq, group * hd), BF16) + 2 * _nbytes((t, hd), BF16)
    temps = 6 * group * _nbytes((tq, tk), F32)
    return pl.pallas_call(
        functools.partial(_flash_kernel, tk=tk, group=group, head_dim=hd),
        grid=(n_seq, cfg.n_kv_heads, t // tq),
        in_specs=in_specs,
        out_specs=pl.BlockSpec((tq, group * hd), lambda b, h, i: (qb0 + b * (t // tq) + i, h)),
        out_shape=jax.ShapeDtypeStruct((n, cfg.n_q_heads * hd), BF16),
        input_output_aliases=aliases,
        compiler_params=pltpu.CompilerParams(
            dimension_semantics=("parallel", "parallel", "arbitrary"),
            vmem_limit_bytes=_vmem_limit(pipelined, temps)),
        name=f"flash_t{t}",
    )(*args)


def _flash_bounded_kernel(q_ref, k_ref, vt_ref, *rest, group, head_dim):
    o_ref, l_ref, acc_ref = rest[-3:]
    tq = q_ref.shape[0]
    n_kb, tk = vt_ref.shape[1], vt_ref.shape[3]
    m = group * tq
    qt = jnp.concatenate([jnp.transpose(q_ref[:, g * head_dim:(g + 1) * head_dim].astype(F32)).astype(BF16)
                          for g in range(group)], axis=1)
    l_ref[...] = jnp.zeros_like(l_ref)
    acc_ref[...] = jnp.zeros_like(acc_ref)

    def body(kb, carry):
        kblk = k_ref[pl.ds(pl.multiple_of(kb * tk, tk), tk), :]
        pt = jnp.exp2(jnp.dot(kblk, qt, preferred_element_type=F32))
        l_ref[...] += jnp.sum(pt.reshape(tk // 8, 8, m), axis=0)
        acc_ref[...] += jnp.dot(vt_ref[0, kb], pt.astype(BF16), preferred_element_type=F32)
        return carry

    lax.fori_loop(0, n_kb, body, 0, unroll=min(4, n_kb))
    ot = acc_ref[...] * (1.0 / jnp.sum(l_ref[...], axis=0, keepdims=True))
    for g in range(group):
        o_ref[:, g * head_dim:(g + 1) * head_dim] = jnp.transpose(ot[:, g * tq:(g + 1) * tq]).astype(o_ref.dtype)


def _flash_bounded_call(qkv, vt, prev_out, *, row0, t, n_seq, tk, cfg):
    n = qkv.shape[0]
    hd, group = cfg.head_dim, cfg.n_q_heads // cfg.n_kv_heads
    tq = _tile(t, 256)
    assert row0 % t == 0 and row0 % tq == 0 and t % tk == 0
    qb0, sb0 = row0 // tq, row0 // t
    k_col0 = cfg.n_q_heads
    in_specs = [
        pl.BlockSpec((tq, group * hd), lambda b, h, i: (qb0 + b * (t // tq) + i, h)),
        pl.BlockSpec((t, hd), lambda b, h, i: (sb0 + b, k_col0 + h)),
        pl.BlockSpec((1, t // tk, hd, tk), lambda b, h, i: (h, sb0 + b, 0, 0)),
    ]
    args = [qkv, qkv, vt]
    aliases = {}
    if prev_out is not None:
        in_specs.append(pl.BlockSpec(memory_space=pl.ANY))
        args.append(prev_out)
        aliases = {3: 0}
    m = group * tq
    pipelined = 2 * _nbytes((tq, group * hd), BF16) + 2 * _nbytes((t, hd), BF16)
    resident = _nbytes((8 + hd, m), F32) + 3 * _nbytes((tk, m), F32)
    return pl.pallas_call(
        functools.partial(_flash_bounded_kernel, group=group, head_dim=hd),
        grid=(n_seq, cfg.n_kv_heads, t // tq),
        in_specs=in_specs,
        out_specs=pl.BlockSpec((tq, group * hd), lambda b, h, i: (qb0 + b * (t // tq) + i, h)),
        out_shape=jax.ShapeDtypeStruct((n, cfg.n_q_heads * hd), BF16),
        scratch_shapes=[pltpu.VMEM((8, m), F32), pltpu.VMEM((hd, m), F32)],
        input_output_aliases=aliases,
        compiler_params=pltpu.CompilerParams(
            dimension_semantics=("parallel", "parallel", "arbitrary"),
            vmem_limit_bytes=_vmem_limit(pipelined, resident)),
        name=f"flash_bounded_t{t}",
    )(*args)


SCORE_BOUND_LOG2 = 60.0


def attention(qkv, score_bound, cfg):
    lens = cfg.seq_lens
    runs, row0, idx = [], 0, 0
    while idx < len(lens):
        t, n_seq = lens[idx], 1
        while idx + n_seq < len(lens) and lens[idx + n_seq] == t:
            n_seq += 1
        runs.append((row0, t, n_seq))
        row0 += t * n_seq
        idx += n_seq

    def out_init(qkv):
        if len(runs) == 1:
            return None
        return jnp.zeros((qkv.shape[0], cfg.n_q_heads * cfg.head_dim), BF16)

    def online(qkv):
        out = out_init(qkv)
        for row0, t, n_seq in runs:
            out = _flash_call(qkv, out, row0=row0, t=t, n_seq=n_seq, cfg=cfg)
        return out

    def bounded(qkv):
        n, hd = qkv.shape[0], cfg.head_dim
        tk = _tile(math.gcd(*lens), 1024)
        nq, nkv = cfg.n_q_heads, cfg.n_kv_heads
        vt = qkv[:, (nq + nkv) * hd:].reshape(n // tk, tk, nkv, hd).transpose(2, 0, 3, 1)
        out = out_init(qkv)
        for row0, t, n_seq in runs:
            out = _flash_bounded_call(qkv, vt, out, row0=row0, t=t, n_seq=n_seq, tk=tk, cfg=cfg)
        return out

    return lax.cond(score_bound <= SCORE_BOUND_LOG2, bounded, online, qkv)


def _log_sigmoid(x):
    return jnp.minimum(x, 0.0) - jnp.log(1.0 + jnp.exp(-jnp.abs(x)))


def _gla_block(fwd, q_ref, k_ref, v_ref, a_ref, wa2, ba, tri_ref, o_ref, st_ref, *, chunk, n_chunks, tau, q_scale):
    dk, dv = st_ref.shape
    rblk = n_chunks * chunk
    log_q_scale = math.log(q_scale)
    q, k, v = q_ref[...], k_ref[...], v_ref[...]
    logits = jnp.dot(a_ref[...].astype(BF16), wa2, preferred_element_type=F32)
    g = _log_sigmoid(logits + ba) * (1.0 / tau)

    g_hi = g.astype(BF16)
    rest = g - g_hi.astype(F32)
    g_mid = rest.astype(BF16)
    g_lo = (rest - g_mid.astype(F32)).astype(BF16)
    tri = tri_ref[...]
    prefix = (jnp.dot(tri, g_hi, preferred_element_type=F32) + jnp.dot(tri, g_mid, preferred_element_type=F32)
              + jnp.dot(tri, g_lo, preferred_element_type=F32))
    totals = [prefix[(c + 1) * chunk - 1:(c + 1) * chunk, :] for c in range(n_chunks)]
    before = [sum(totals[:c], jnp.zeros_like(totals[0])) for c in range(n_chunks)]
    total = before[-1] + totals[-1]

    local, ahead, cum = [], [], []
    for c in range(n_chunks):
        rows = slice(c * chunk, (c + 1) * chunk)
        loc = prefix[rows] if fwd else totals[c] - prefix[rows] + g[rows]
        ahd = before[c] if fwd else total - before[c] - totals[c]
        local.append(loc)
        ahead.append(ahd)
        cum.append(loc + ahd)
    cum = jnp.concatenate(cum, axis=0)

    st = st_ref[...]
    q_loc = [q[c * chunk:(c + 1) * chunk] * jnp.exp(local[c] + log_q_scale) for c in range(n_chunks)]
    q_blk = jnp.concatenate([q_loc[c] * jnp.exp(ahead[c]) for c in range(n_chunks)], axis=0).astype(BF16)
    o = jnp.dot(q_blk, st.astype(BF16), preferred_element_type=F32)

    attn = []
    for c in range(n_chunks):
        k_ref_c = (k * jnp.exp(ahead[c] - cum)).astype(BF16)
        attn.append(lax.dot_general(q_loc[c].astype(BF16), k_ref_c, (((1,), (1,)), ((), ())),
                                    preferred_element_type=F32))
    attn = jnp.concatenate(attn, axis=0)
    row = lax.broadcasted_iota(jnp.int32, (rblk, rblk), 0)
    col = lax.broadcasted_iota(jnp.int32, (rblk, rblk), 1)
    mask = (row >= col) if fwd else (col > row)
    o += jnp.dot(jnp.where(mask, attn, 0.0).astype(BF16), v, preferred_element_type=F32)
    o_ref[...] = o

    k_dec = (k * jnp.exp(total - cum)).astype(BF16)
    dec = jnp.transpose(jnp.broadcast_to(jnp.exp(total), (V7X_LANES, dk)))
    dec = jnp.concatenate([dec] * (dv // V7X_LANES), axis=1)
    st_ref[...] = st * dec + lax.dot_general(k_dec, v, (((0,), (0,)), ((), ())), preferred_element_type=F32)


def _gla_kernel(starts_ref, ends_ref, qf_ref, kf_ref, vf_ref, af_ref, qb_ref, kb_ref, vb_ref, ab_ref,
                wa2_ref, ba_ref, tri_ref, of_ref, ob_ref, st_ref, *, n_blocks, **block_args):
    i = pl.program_id(1)

    @pl.when(starts_ref[i] == 1)
    def _():
        st_ref[0] = jnp.zeros(st_ref.shape[1:], st_ref.dtype)

    @pl.when(ends_ref[n_blocks - 1 - i] == 1)
    def _():
        st_ref[1] = jnp.zeros(st_ref.shape[1:], st_ref.dtype)

    _gla_block(True, qf_ref, kf_ref, vf_ref, af_ref, wa2_ref[0], ba_ref[0], tri_ref, of_ref, st_ref.at[0],
               **block_args)
    _gla_block(False, qb_ref, kb_ref, vb_ref, ab_ref, wa2_ref[1], ba_ref[1], tri_ref, ob_ref, st_ref.at[1],
               **block_args)


def gla_scan(qk, v, a, wa2, ba, cfg):
    n = qk.shape[0]
    nh, dk, dv, chunk = cfg.gla_heads, cfg.gla_dk, cfg.gla_dv, cfg.gla_chunk
    rblk = _tile(math.gcd(*cfg.seq_lens), 256)
    n_blocks = n // rblk
    starts, ends, row = [0] * n_blocks, [0] * n_blocks, 0
    for t in cfg.seq_lens:
        starts[row // rblk] = 1
        row += t
        ends[row // rblk - 1] = 1
    starts = jnp.asarray(starts, jnp.int32)
    ends = jnp.asarray(ends, jnp.int32)
    r_idx = jnp.arange(rblk)
    tri = ((r_idx[:, None] // chunk == r_idx[None, :] // chunk) & (r_idx[None, :] <= r_idx[:, None])).astype(BF16)

    def streams(row_block):
        return [pl.BlockSpec((rblk, dk), lambda h, i, s, e: (row_block(i), h)),
                pl.BlockSpec((rblk, dk), lambda h, i, s, e: (row_block(i), nh + h)),
                pl.BlockSpec((rblk, dv), lambda h, i, s, e: (row_block(i), h)),
                pl.BlockSpec((rblk, V7X_LANES), lambda h, i, s, e: (row_block(i), 0))]

    fwd_block = lambda i: i
    bwd_block = lambda i: n_blocks - 1 - i
    grid_spec = pltpu.PrefetchScalarGridSpec(
        num_scalar_prefetch=2,
        grid=(nh, n_blocks),
        in_specs=streams(fwd_block) + streams(bwd_block) + [
            pl.BlockSpec((2, V7X_LANES, dk), lambda h, i, s, e: (0, 0, h)),
            pl.BlockSpec((2, 1, dk), lambda h, i, s, e: (0, 0, h)),
            pl.BlockSpec((rblk, rblk), lambda h, i, s, e: (0, 0)),
        ],
        out_specs=[pl.BlockSpec((rblk, dv), lambda h, i, s, e: (fwd_block(i), h)),
                   pl.BlockSpec((rblk, dv), lambda h, i, s, e: (bwd_block(i), h))],
        scratch_shapes=[pltpu.VMEM((2, dk, dv), F32)],
    )
    pipelined = 2 * (2 * _nbytes((rblk, dk), F32) + _nbytes((rblk, dv), BF16) + _nbytes((rblk, V7X_LANES), F32)
                     + _nbytes((V7X_LANES, dk), BF16) + _nbytes((rblk, dv), F32))
    return pl.pallas_call(
        functools.partial(_gla_kernel, chunk=chunk, n_chunks=rblk // chunk, n_blocks=n_blocks,
                          tau=cfg.gla_tau, q_scale=dk ** -0.5),
        grid_spec=grid_spec,
        out_shape=[jax.ShapeDtypeStruct((n, nh * dv), F32)] * 2,
        compiler_params=pltpu.CompilerParams(
            dimension_semantics=("parallel", "arbitrary"),
            vmem_limit_bytes=_vmem_limit(pipelined, 8 * _nbytes((dv, dk), F32))),
        name="gla_scan",
    )(starts, ends, qk, qk, v, a, qk, qk, v, a, wa2, ba, tri)


def _trunk(xa, xb, p, cfg):
    d = cfg.d_model
    hd = cfg.head_dim
    na, n = xa.shape[0], xa.shape[0] + xb.shape[0]
    bf = lambda w: w.astype(BF16)
    TM = _tile(math.gcd(na, n - na), 1024)
    a_blocks = na // TM

    h = rmsnorm_pair(xa, xb, p["norm_mix"][0], BF16, cfg)
    w_qkv = bf(jnp.concatenate([p["attn_wq"][0], p["attn_wk"][0], p["attn_wv"][0]], axis=1))
    cos, sin_lo, sin_hi = _rope_tables(cfg)
    tn_qkv = math.gcd(4 * hd, cfg.n_kv_heads * hd)
    tab = lambda arr: (arr, (TM if n >= TM else n, hd), lambda i, j: (i, 0))
    vec = lambda arr: (arr.reshape(1, -1), (1, arr.size), lambda i, j: (0, 0))
    q_scale = hd ** -0.5 * math.log2(math.e)
    qkv = matmul(
        h, [w_qkv],
        functools.partial(_epi_qkv, n_q_blocks=cfg.n_q_heads * hd // tn_qkv,
                          n_k_blocks=cfg.n_kv_heads * hd // tn_qkv, head_dim=hd, eps=cfg.eps),
        out_cols=w_qkv.shape[1], out_dtype=BF16, tm=TM, tn=tn_qkv, tk=d,
        extras=[tab(cos), tab(sin_lo), tab(sin_hi),
                vec(p["attn_q_norm"][0] * q_scale), vec(p["attn_k_norm"][0])],
        name="qkv_proj")
    score_bound = (hd * jnp.max(jnp.abs(p["attn_q_norm"][0] * q_scale)) * jnp.max(jnp.abs(p["attn_k_norm"][0]))
                   * (1.0 + 2.0 ** -6))
    o = attention(qkv, score_bound, cfg)
    res = lambda arr, tn: (arr, (TM, tn), lambda i, j: (i, j))
    x = matmul(o, [bf(p["attn_wo"][0])], functools.partial(_epi_residual_pair, a_blocks=a_blocks),
               out_cols=d, out_dtype=F32, tm=TM, tn=512, tk=o.shape[1],
               extras=[(xa, (TM, 512), lambda i, j: (jnp.minimum(i, a_blocks - 1),
                                                     jnp.where(i < a_blocks, j, d // 512 - 1))),
                       (xb, (TM, 512), lambda i, j: (jnp.maximum(i - a_blocks, 0),
                                                     jnp.where(i < a_blocks, 0, j)))],
               name="attn_out")

    h = rmsnorm(x, p["norm_ffn"][0], BF16, cfg)
    hid = matmul(h, [bf(p["ffn_w1"][0]), bf(p["ffn_w3"][0])], _epi_swiglu,
                 out_cols=cfg.d_ff, out_dtype=BF16, tm=TM, tn=512, tk=d, name="ffn_up")
    x = matmul(hid, [bf(p["ffn_w2"][0])], _epi_residual, out_cols=d, out_dtype=F32,
               tm=TM, tn=1024, tk=2048, extras=[res(x, 1024)], name="ffn_down")

    nh, dk, dv, rank = cfg.gla_heads, cfg.gla_dk, cfg.gla_dv, cfg.gla_rank
    h = rmsnorm(x, p["norm_mix"][1], BF16, cfg)
    qk = matmul(h, [bf(jnp.concatenate([p["gla_wq"][0], p["gla_wk"][0]], axis=1))], _epi_store,
                out_cols=2 * nh * dk, out_dtype=F32, tm=TM, tn=1024, tk=d, name="gla_qk")
    v = matmul(h, [bf(p["gla_wv"][0])], _epi_store, out_cols=nh * dv, out_dtype=BF16,
               tm=TM, tn=1024, tk=d, name="gla_v")
    wa1 = jnp.zeros((d, V7X_LANES), F32)
    wa1 = wa1.at[:, :rank].set(p["gla_wa1_f"][0]).at[:, rank:2 * rank].set(p["gla_wa1_b"][0])
    a = matmul(h, [bf(wa1)], _epi_store, out_cols=V7X_LANES, out_dtype=F32,
               tm=TM, tn=V7X_LANES, tk=d, name="gla_gate_lowrank")
    wa2 = jnp.zeros((2, V7X_LANES, nh * dk), F32)
    wa2 = wa2.at[0, :rank].set(p["gla_wa2_f"][0]).at[1, rank:2 * rank].set(p["gla_wa2_b"][0])
    ba = jnp.stack([p["gla_ba_f"][0], p["gla_ba_b"][0]]).reshape(2, 1, nh * dk)
    o_f, o_b = gla_scan(qk, v, a, bf(wa2), ba, cfg)
    gated = matmul(
        h, [bf(p["gla_wg"][0])], functools.partial(_epi_gla_gate, eps=cfg.eps),
        out_cols=nh * dv, out_dtype=BF16, tm=TM // 2, tn=dv, tk=d,
        extras=[(o_f, (TM // 2, dv), lambda i, j: (i, j)), (o_b, (TM // 2, dv), lambda i, j: (i, j)),
                (jnp.tile(p["gla_o_norm"][0], nh).reshape(1, nh * dv), (1, dv), lambda i, j: (0, j))],
        name="gla_gate")
    x = matmul(gated, [bf(p["gla_wo"][0])], _epi_residual, out_cols=d, out_dtype=F32,
               tm=TM, tn=1024, tk=nh * dv, extras=[res(x, 1024)], name="gla_out")

    route = router(x, p["norm_ffn"][1], p["moe_router"][0], cfg)
    return routed_moe_final(x, route, p["norm_ffn"][1], bf(p["moe_w1"][0]), bf(p["moe_w3"][0]),
                            bf(p["moe_w2"][0]), p["norm_final"], na, cfg)


def kernel(x_prompt, x_sample, norm_mix, norm_ffn, norm_final, attn_wq, attn_wk, attn_wv, attn_q_norm, attn_k_norm, attn_wo, gla_wq, gla_wk, gla_wv, gla_wg, gla_wa1_f, gla_wa2_f, gla_ba_f, gla_wa1_b, gla_wa2_b, gla_ba_b, gla_o_norm, gla_wo, ffn_w1, ffn_w3, ffn_w2, moe_router, moe_w1, moe_w3, moe_w2):
    cfg = PROD
    d = cfg.d_model
    params = dict(
        norm_mix=norm_mix, norm_ffn=norm_ffn, norm_final=norm_final,
        attn_wq=attn_wq, attn_wk=attn_wk, attn_wv=attn_wv, attn_q_norm=attn_q_norm,
        attn_k_norm=attn_k_norm, attn_wo=attn_wo,
        gla_wq=gla_wq, gla_wk=gla_wk, gla_wv=gla_wv, gla_wg=gla_wg,
        gla_wa1_f=gla_wa1_f, gla_wa2_f=gla_wa2_f, gla_ba_f=gla_ba_f,
        gla_wa1_b=gla_wa1_b, gla_wa2_b=gla_wa2_b, gla_ba_b=gla_ba_b,
        gla_o_norm=gla_o_norm, gla_wo=gla_wo,
        ffn_w1=ffn_w1, ffn_w3=ffn_w3, ffn_w2=ffn_w2,
        moe_router=moe_router, moe_w1=moe_w1, moe_w3=moe_w3, moe_w2=moe_w2)
    y_prompt, y_sample = _trunk(x_prompt.reshape(-1, d), x_sample.reshape(-1, d), params, cfg)
    return (y_prompt.reshape(x_prompt.shape), y_sample.reshape(x_sample.shape))
```

```python
import functools
import math
from typing import NamedTuple

import jax
import jax.numpy as jnp
from jax import lax
from jax.experimental import pallas as pl
from jax.experimental.pallas import tpu as pltpu

F32 = jnp.float32
BF16 = jnp.bfloat16

V7X_VMEM_BYTES = 64 * 1024 * 1024
V7X_LANES = 128
VMEM_CAP_BYTES = V7X_VMEM_BYTES - 8 * 1024 * 1024


class Cfg(NamedTuple):
    d_model: int
    seq_lens: tuple
    grid_w: int
    head_dim: int
    n_q_heads: int
    n_kv_heads: int
    rope_theta: float
    gla_heads: int
    gla_dk: int
    gla_dv: int
    gla_rank: int
    gla_tau: float
    gla_chunk: int
    d_ff: int
    n_experts: int
    moe_d_ff: int
    eps: float


PROD = Cfg(
    d_model=4096, seq_lens=(16384, 2048, 2048, 2048, 2048), grid_w=64,
    head_dim=128, n_q_heads=32, n_kv_heads=8, rope_theta=10000.0,
    gla_heads=4, gla_dk=512, gla_dv=1024, gla_rank=16, gla_tau=16.0, gla_chunk=64,
    d_ff=8192, n_experts=8, moe_d_ff=1024, eps=1e-6)


def _vmem_limit(pipelined_bytes, resident_bytes):
    need = 2 * pipelined_bytes + resident_bytes
    return int(min(VMEM_CAP_BYTES, max(need, 16 * 1024 * 1024)))


def _nbytes(shape, dtype):
    return math.prod(shape) * jnp.dtype(dtype).itemsize


def _tile(n, pref):
    t = min(n, pref)
    assert n % t == 0, (n, pref)
    return t


def _rms(x, eps):
    return x * lax.rsqrt(jnp.mean(x * x, axis=-1, keepdims=True) + eps)


def _rmsnorm_kernel(x_ref, w_ref, o_ref, *, eps):
    o_ref[...] = (_rms(x_ref[...], eps) * w_ref[...]).astype(o_ref.dtype)


def rmsnorm(x, w, out_dtype, cfg):
    n, d = x.shape
    tm = _tile(n, 256)
    return pl.pallas_call(
        functools.partial(_rmsnorm_kernel, eps=cfg.eps),
        grid=(n // tm,),
        in_specs=[pl.BlockSpec((tm, d), lambda i: (i, 0)),
                  pl.BlockSpec((1, d), lambda i: (0, 0))],
        out_specs=pl.BlockSpec((tm, d), lambda i: (i, 0)),
        out_shape=jax.ShapeDtypeStruct((n, d), out_dtype),
        compiler_params=pltpu.CompilerParams(
            dimension_semantics=("parallel",),
            vmem_limit_bytes=_vmem_limit(_nbytes((tm, d), F32) + _nbytes((tm, d), out_dtype),
                                         2 * _nbytes((tm, d), F32))),
        name="rmsnorm",
    )(x, w.reshape(1, d))


def _rmsnorm_pair_kernel(xa_ref, xb_ref, w_ref, o_ref, *, eps, a_blocks):
    i = pl.program_id(0)

    @pl.when(i < a_blocks)
    def _():
        o_ref[...] = (_rms(xa_ref[...], eps) * w_ref[...]).astype(o_ref.dtype)

    @pl.when(i >= a_blocks)
    def _():
        o_ref[...] = (_rms(xb_ref[...], eps) * w_ref[...]).astype(o_ref.dtype)


def rmsnorm_pair(xa, xb, w, out_dtype, cfg):
    na, nb, d = xa.shape[0], xb.shape[0], xa.shape[1]
    tm = _tile(math.gcd(na, nb), 256)
    a_blocks = na // tm
    return pl.pallas_call(
        functools.partial(_rmsnorm_pair_kernel, eps=cfg.eps, a_blocks=a_blocks),
        grid=((na + nb) // tm,),
        in_specs=[pl.BlockSpec((tm, d), lambda i: (jnp.minimum(i, a_blocks - 1), 0)),
                  pl.BlockSpec((tm, d), lambda i: (jnp.maximum(i - a_blocks, 0), 0)),
                  pl.BlockSpec((1, d), lambda i: (0, 0))],
        out_specs=pl.BlockSpec((tm, d), lambda i: (i, 0)),
        out_shape=jax.ShapeDtypeStruct((na + nb, d), out_dtype),
        compiler_params=pltpu.CompilerParams(
            dimension_semantics=("arbitrary",),
            vmem_limit_bytes=_vmem_limit(2 * _nbytes((tm, d), F32) + _nbytes((tm, d), out_dtype),
                                         2 * _nbytes((tm, d), F32))),
        name="rmsnorm_pair",
    )(xa, xb, w.reshape(1, d))


def _router_kernel(x_ref, w_ref, r_ref, route_ref, *, eps, n_experts):
    h = _rms(x_ref[...], eps) * w_ref[...]
    r = r_ref[...]
    h_hi, r_hi = h.astype(BF16), r.astype(BF16)
    h_lo, r_lo = (h - h_hi.astype(F32)).astype(BF16), (r - r_hi.astype(F32)).astype(BF16)
    logits = (jnp.dot(h_hi, r_hi, preferred_element_type=F32) + jnp.dot(h_lo, r_hi, preferred_element_type=F32)
              + jnp.dot(h_hi, r_lo, preferred_element_type=F32))
    lane = lax.broadcasted_iota(jnp.int32, logits.shape, 1)
    neg = jnp.float32(-jnp.inf)
    logits = jnp.where(lane < n_experts, logits, neg)
    v1 = jnp.max(logits, axis=-1, keepdims=True)
    i1 = jnp.min(jnp.where(logits == v1, lane, V7X_LANES), axis=-1, keepdims=True)
    rest = jnp.where(lane == i1, neg, logits)
    v2 = jnp.max(rest, axis=-1, keepdims=True)
    i2 = jnp.min(jnp.where(rest == v2, lane, V7X_LANES), axis=-1, keepdims=True)
    e2 = jnp.exp(v2 - v1)
    denom = 1.0 + e2
    route = jnp.where(lane == 0, i1.astype(F32), 0.0) + jnp.where(lane == 1, i2.astype(F32), 0.0)
    route_ref[...] = route + jnp.where(lane == 2, 1.0 / denom, 0.0) + jnp.where(lane == 3, e2 / denom, 0.0)


def router(x, w, router_w, cfg):
    n, d = x.shape
    tm = _tile(n, 256)
    r_pad = jnp.zeros((d, V7X_LANES), F32).at[:, :cfg.n_experts].set(router_w)
    return pl.pallas_call(
        functools.partial(_router_kernel, eps=cfg.eps, n_experts=cfg.n_experts),
        grid=(n // tm,),
        in_specs=[pl.BlockSpec((tm, d), lambda i: (i, 0)),
                  pl.BlockSpec((1, d), lambda i: (0, 0)),
                  pl.BlockSpec((d, V7X_LANES), lambda i: (0, 0))],
        out_specs=pl.BlockSpec((tm, V7X_LANES), lambda i: (i, 0)),
        out_shape=jax.ShapeDtypeStruct((n, V7X_LANES), F32),
        compiler_params=pltpu.CompilerParams(
            dimension_semantics=("parallel",),
            vmem_limit_bytes=_vmem_limit(_nbytes((tm, d), F32) + _nbytes((d, V7X_LANES), F32),
                                         4 * _nbytes((tm, d), F32))),
        name="router",
    )(x, w.reshape(1, d), r_pad)


def _row_copy(src_hbm, dst_vmem, sem, src_row, dst_row):
    return pltpu.make_async_copy(src_hbm.at[pl.ds(src_row, 1), :], dst_vmem.at[pl.ds(dst_row, 1), :], sem)


def _gather_rows(src_hbm, dst_vmem, sem, row_of, n_rows):
    def start(r, c):
        _row_copy(src_hbm, dst_vmem, sem, row_of(r), r).start()
        return c

    lax.fori_loop(0, n_rows, start, 0, unroll=8)


def _wait_rows(src_hbm, dst_vmem, sem, n_rows):
    def wait(r, c):
        _row_copy(src_hbm, dst_vmem, sem, 0, r).wait()
        return c

    lax.fori_loop(0, n_rows, wait, 0, unroll=8)


def _moe_up_kernel(tile_expert_ref, tile_src_ref, tok_ref, x_hbm, nw_ref, w1_ref, w3_ref, hid_ref,
                   xbuf, xn_ref, sem, *, tm, n_tiles, eps):
    i = pl.program_id(0)
    cur = lax.rem(i, 2)
    nxt_tile = jnp.minimum(i + 1, n_tiles - 1)
    half = hid_ref.shape[1] // 2

    def gather(tile, buf):
        src0 = tile_src_ref[tile]
        _gather_rows(x_hbm, xbuf.at[buf], sem.at[buf], lambda r: tok_ref[src0 + r], tm)

    def normalise(buf):
        xn_ref[buf] = (_rms(xbuf[buf], eps) * nw_ref[...]).astype(xn_ref.dtype)

    @pl.when(i == 0)
    def _():
        gather(0, 0)
        _wait_rows(x_hbm, xbuf.at[0], sem.at[0], tm)
        normalise(0)

    def up_half(c):
        xn = xn_ref[cur]
        cols = slice(c * half, (c + 1) * half)
        a1 = jnp.dot(xn, w1_ref[0, :, cols], preferred_element_type=F32)
        a3 = jnp.dot(xn, w3_ref[0, :, cols], preferred_element_type=F32)
        hid_ref[:, cols] = (_silu(a1) * a3).astype(hid_ref.dtype)

    gather(nxt_tile, 1 - cur)
    up_half(0)
    _wait_rows(x_hbm, xbuf.at[1 - cur], sem.at[1 - cur], tm)
    up_half(1)
    normalise(1 - cur)


def _moe_down_kernel(tile_expert_ref, hid_ref, w2_ref, y_ref):
    y_ref[...] = jnp.dot(hid_ref[...], w2_ref[0], preferred_element_type=F32)


def _moe_combine_kernel(pos_ref, x_ref, route_ref, nw_ref, y_hbm, oa_ref, ob_ref, ybuf, sem,
                        *, tb, n_tokens, n_steps, a_steps, top_k, eps):
    i = pl.program_id(0)
    cur = lax.rem(i, 2)

    def gather(step, buf):
        base = step * tb
        for s in range(top_k):
            _gather_rows(y_hbm, ybuf.at[buf, s], sem.at[buf], lambda r, s=s: pos_ref[s * n_tokens + base + r], tb)

    @pl.when(i == 0)
    def _():
        gather(0, 0)

    @pl.when(i + 1 < n_steps)
    def _():
        gather(i + 1, 1 - cur)

    for s in range(top_k):
        _wait_rows(y_hbm, ybuf.at[cur, s], sem.at[cur], tb)
    acc = x_ref[...]
    route = route_ref[...]
    for s in range(top_k):
        acc = acc + ybuf[cur, s] * route[:, top_k + s:top_k + s + 1]
    out = _rms(acc, eps) * nw_ref[...]

    @pl.when(i < a_steps)
    def _():
        oa_ref[...] = out

    @pl.when(i >= a_steps)
    def _():
        ob_ref[...] = out


def _route_plan(route, tm, cfg):
    n, ne, top_k = route.shape[0], cfg.n_experts, 2
    pairs = top_k * n
    expert = route[:, :top_k].astype(jnp.int32).T.reshape(-1)
    token = jnp.tile(jnp.arange(n, dtype=jnp.int32), top_k)
    index = jnp.arange(pairs, dtype=jnp.int32)
    sorted_expert, sorted_index, sorted_token = lax.sort((expert, index, token), num_keys=1)
    onehot = sorted_expert[:, None] == jnp.arange(ne, dtype=jnp.int32)[None, :]
    counts = jnp.sum(onehot, axis=0).astype(jnp.int32)
    padded = ((counts + tm - 1) // tm) * tm
    pad_end = jnp.cumsum(padded)
    shift = (pad_end - padded) - (jnp.cumsum(counts) - counts)
    dest_sorted = index + jnp.sum(jnp.where(onehot, shift[None, :], 0), axis=1)
    _, pos = lax.sort((sorted_index, dest_sorted), num_keys=1)
    n_tiles = (pairs + ne * tm) // tm
    tile_start = jnp.arange(n_tiles, dtype=jnp.int32) * tm
    tile_expert = jnp.minimum(jnp.sum(tile_start[:, None] >= pad_end[None, :], axis=1), ne - 1).astype(jnp.int32)
    tile_shift = jnp.sum(jnp.where(tile_expert[:, None] == jnp.arange(ne)[None, :], shift[None, :], 0), axis=1)
    tile_src = jnp.clip(tile_start - tile_shift, 0, pairs - 1).astype(jnp.int32)
    tok = jnp.concatenate([sorted_token, jnp.zeros((tm,), jnp.int32)])
    return tile_expert, tile_src, tok, pos


def routed_moe_final(x, route, norm_w, w1, w3, w2, final_w, n_first, cfg):
    n, d = x.shape
    ne, eff, top_k = cfg.n_experts, cfg.moe_d_ff, 2
    tm = _tile(n, 256)
    tile_expert, tile_src, tok, pos = _route_plan(route, tm, cfg)
    n_tiles = tile_expert.shape[0]
    p_rows = n_tiles * tm

    hid = pl.pallas_call(
        functools.partial(_moe_up_kernel, tm=tm, n_tiles=n_tiles, eps=cfg.eps),
        grid_spec=pltpu.PrefetchScalarGridSpec(
            num_scalar_prefetch=3,
            grid=(n_tiles,),
            in_specs=[
                pl.BlockSpec(memory_space=pl.ANY),
                pl.BlockSpec((1, d), lambda i, te, ts, tk: (0, 0)),
                pl.BlockSpec((1, d, eff), lambda i, te, ts, tk: (te[i], 0, 0)),
                pl.BlockSpec((1, d, eff), lambda i, te, ts, tk: (te[i], 0, 0)),
            ],
            out_specs=pl.BlockSpec((tm, eff), lambda i, te, ts, tk: (i, 0)),
            scratch_shapes=[pltpu.VMEM((2, tm, d), F32), pltpu.VMEM((2, tm, d), BF16),
                            pltpu.SemaphoreType.DMA((2,))],
        ),
        out_shape=jax.ShapeDtypeStruct((p_rows, eff), BF16),
        compiler_params=pltpu.CompilerParams(
            dimension_semantics=("arbitrary",),
            vmem_limit_bytes=_vmem_limit(
                2 * _nbytes((d, eff), BF16) + _nbytes((tm, eff), BF16),
                3 * _nbytes((tm, d), F32) + 2 * _nbytes((tm, d), BF16) + 3 * _nbytes((tm, eff), F32))),
        name="moe_up",
    )(tile_expert, tile_src, tok, x, norm_w.reshape(1, d), w1, w3)

    tn2 = _tile(d, 2048)
    y = pl.pallas_call(
        _moe_down_kernel,
        grid_spec=pltpu.PrefetchScalarGridSpec(
            num_scalar_prefetch=1,
            grid=(d // tn2, n_tiles),
            in_specs=[
                pl.BlockSpec((tm, eff), lambda j, i, te: (i, 0)),
                pl.BlockSpec((1, eff, tn2), lambda j, i, te: (te[i], 0, j)),
            ],
            out_specs=pl.BlockSpec((tm, tn2), lambda j, i, te: (i, j)),
        ),
        out_shape=jax.ShapeDtypeStruct((p_rows, d), F32),
        compiler_params=pltpu.CompilerParams(
            dimension_semantics=("parallel", "parallel"),
            vmem_limit_bytes=_vmem_limit(
                _nbytes((tm, eff), BF16) + _nbytes((eff, tn2), BF16) + _nbytes((tm, tn2), F32),
                2 * _nbytes((tm, tn2), F32))),
        name="moe_down",
    )(tile_expert, hid, w2)

    tb = _tile(math.gcd(n_first, n - n_first), 256)
    n_steps, a_steps = n // tb, n_first // tb
    return pl.pallas_call(
        functools.partial(_moe_combine_kernel, tb=tb, n_tokens=n, n_steps=n_steps, a_steps=a_steps,
                          top_k=top_k, eps=cfg.eps),
        grid_spec=pltpu.PrefetchScalarGridSpec(
            num_scalar_prefetch=1,
            grid=(n_steps,),
            in_specs=[
                pl.BlockSpec((tb, d), lambda i, ps: (i, 0)),
                pl.BlockSpec((tb, V7X_LANES), lambda i, ps: (i, 0)),
                pl.BlockSpec((1, d), lambda i, ps: (0, 0)),
                pl.BlockSpec(memory_space=pl.ANY),
            ],
            out_specs=[pl.BlockSpec((tb, d), lambda i, ps: (jnp.minimum(i, a_steps - 1), 0)),
                       pl.BlockSpec((tb, d), lambda i, ps: (jnp.maximum(i - a_steps, 0), 0))],
            scratch_shapes=[pltpu.VMEM((2, top_k, tb, d), F32), pltpu.SemaphoreType.DMA((2,))],
        ),
        out_shape=[jax.ShapeDtypeStruct((n_first, d), F32), jax.ShapeDtypeStruct((n - n_first, d), F32)],
        compiler_params=pltpu.CompilerParams(
            dimension_semantics=("arbitrary",),
            vmem_limit_bytes=_vmem_limit(3 * _nbytes((tb, d), F32), (2 * top_k + 3) * _nbytes((tb, d), F32))),
        name="moe_combine_final_norm",
    )(pos, x, route, final_w.reshape(1, d), y)


def _mm_kernel(*refs, n_w, n_extra, nk, epilogue):
    a_ref = refs[0]
    w_refs = refs[1:1 + n_w]
    extra_refs = refs[1 + n_w:1 + n_w + n_extra]
    o_ref = refs[1 + n_w + n_extra]
    acc_refs = refs[2 + n_w + n_extra:]
    a = a_ref[...]
    parts = [jnp.dot(a, w[...], preferred_element_type=F32) for w in w_refs]
    if nk == 1:
        epilogue(parts, extra_refs, o_ref)
        return
    k = pl.program_id(2)

    @pl.when(k == 0)
    def _():
        for acc, p in zip(acc_refs, parts):
            acc[...] = p

    @pl.when(jnp.logical_and(k > 0, k < nk - 1))
    def _():
        for acc, p in zip(acc_refs, parts):
            acc[...] += p

    @pl.when(k == nk - 1)
    def _():
        epilogue([acc[...] + p for acc, p in zip(acc_refs, parts)], extra_refs, o_ref)


def matmul(a, ws, epilogue, *, out_cols, out_dtype, tm, tn, tk, extras=(), name):
    m, kdim = a.shape
    n = ws[0].shape[1]
    tm, tn, tk = _tile(m, tm), _tile(n, tn), _tile(kdim, tk)
    nk = kdim // tk
    assert out_cols == n
    in_specs = [pl.BlockSpec((tm, tk), lambda i, j, k: (i, k))]
    in_specs += [pl.BlockSpec((tk, tn), lambda i, j, k: (k, j)) for _ in ws]
    pipelined = _nbytes((tm, tk), a.dtype) + len(ws) * _nbytes((tk, tn), ws[0].dtype)
    for arr, blk, imap in extras:
        in_specs.append(pl.BlockSpec(blk, lambda i, j, k, imap=imap: imap(i, j)))
        pipelined += _nbytes(blk, arr.dtype)
    pipelined += _nbytes((tm, tn), out_dtype)
    acc_bytes = len(ws) * _nbytes((tm, tn), F32)
    scratch = [pltpu.VMEM((tm, tn), F32) for _ in ws] if nk > 1 else []
    return pl.pallas_call(
        functools.partial(_mm_kernel, n_w=len(ws), n_extra=len(extras), nk=nk, epilogue=epilogue),
        grid=(m // tm, n // tn, nk),
        in_specs=in_specs,
        out_specs=pl.BlockSpec((tm, tn), lambda i, j, k: (i, j)),
        out_shape=jax.ShapeDtypeStruct((m, out_cols), out_dtype),
        scratch_shapes=scratch,
        compiler_params=pltpu.CompilerParams(
            dimension_semantics=("parallel", "parallel", "arbitrary"),
            vmem_limit_bytes=_vmem_limit(pipelined, 3 * acc_bytes)),
        name=name,
    )(a, *ws, *[e[0] for e in extras])


def _epi_store(parts, extra_refs, o_ref):
    o_ref[...] = parts[0].astype(o_ref.dtype)


def _epi_residual(parts, extra_refs, o_ref):
    o_ref[...] = (extra_refs[0][...] + parts[0]).astype(o_ref.dtype)


def _epi_residual_pair(parts, extra_refs, o_ref, *, a_blocks):
    i = pl.program_id(0)

    @pl.when(i < a_blocks)
    def _():
        o_ref[...] = (extra_refs[0][...] + parts[0]).astype(o_ref.dtype)

    @pl.when(i >= a_blocks)
    def _():
        o_ref[...] = (extra_refs[1][...] + parts[0]).astype(o_ref.dtype)


def _silu(x):
    return x * (1.0 / (1.0 + jnp.exp(-x)))


def _epi_swiglu(parts, extra_refs, o_ref):
    o_ref[...] = (_silu(parts[0]) * parts[1]).astype(o_ref.dtype)


def _epi_qkv(parts, extra_refs, o_ref, *, n_q_blocks, n_k_blocks, head_dim, eps):
    acc = parts[0]
    cos_ref, sin_lo_ref, sin_hi_ref, qw_ref, kw_ref = extra_refs
    j = pl.program_id(1)
    heads = acc.shape[1] // head_dim

    def norm_rope(w):
        cos, sin_lo, sin_hi = cos_ref[...], sin_lo_ref[...], sin_hi_ref[...]
        for g in range(heads):
            sl = slice(g * head_dim, (g + 1) * head_dim)
            y = _rms(acc[:, sl], eps) * w
            r = (y * cos + pltpu.roll(y, head_dim - head_dim // 4, 1) * sin_lo
                 + pltpu.roll(y, head_dim // 4, 1) * sin_hi)
            o_ref[:, sl] = r.astype(o_ref.dtype)

    @pl.when(j < n_q_blocks)
    def _():
        norm_rope(qw_ref[...])

    @pl.when(jnp.logical_and(j >= n_q_blocks, j < n_q_blocks + n_k_blocks))
    def _():
        norm_rope(kw_ref[...])

    @pl.when(j >= n_q_blocks + n_k_blocks)
    def _():
        o_ref[...] = acc.astype(o_ref.dtype)


def _epi_gla_gate(parts, extra_refs, o_ref, *, eps):
    of_ref, ob_ref, w_ref = extra_refs
    o = of_ref[...] + ob_ref[...]
    o_ref[...] = (_rms(o, eps) * w_ref[...] * _silu(parts[0])).astype(o_ref.dtype)


def _rope_tables(cfg):
    hd = cfg.head_dim
    axis_dim = hd // 2
    inv_freq = cfg.rope_theta ** (-jnp.arange(0, axis_dim, 2, dtype=F32) / axis_dim)
    tabs = []
    for t in cfg.seq_lens:
        pos = jnp.arange(t)
        ang_r = (pos // cfg.grid_w).astype(F32)[:, None] * inv_freq
        ang_c = (pos % cfg.grid_w).astype(F32)[:, None] * inv_freq
        tabs.append(jnp.concatenate([ang_r, ang_r, ang_c, ang_c], axis=-1))
    ang = jnp.concatenate(tabs, axis=0)
    cos, sin = jnp.cos(ang), jnp.sin(ang)
    first_half = (jnp.arange(hd) % (hd // 2)) < (hd // 4)
    sin_lo = jnp.where(first_half, -sin, 0.0)
    sin_hi = jnp.where(first_half, 0.0, sin)
    return cos, sin_lo, sin_hi


def _flash_kernel(q_ref, k_ref, v_ref, *rest, tk, group, head_dim):
    o_ref = rest[-1]
    t = k_ref.shape[0]
    tq = q_ref.shape[0]
    qs = [q_ref[:, g * head_dim:(g + 1) * head_dim] for g in range(group)]

    def body(kb, carry):
        off = pl.multiple_of(kb * tk, tk)
        kblk = k_ref[pl.ds(off, tk), :]
        vblk = v_ref[pl.ds(off, tk), :]
        out = []
        for g in range(group):
            m_prev, l_prev, acc_prev = carry[g]
            s = lax.dot_general(qs[g], kblk, (((1,), (1,)), ((), ())), preferred_element_type=F32)
            m_new = jnp.maximum(m_prev, jnp.max(s, axis=-1, keepdims=True))
            p = jnp.exp2(s - m_new)
            alpha = jnp.exp2(m_prev - m_new)
            l_new = alpha * l_prev + jnp.sum(p, axis=-1, keepdims=True)
            acc_new = alpha * acc_prev + jnp.dot(p.astype(BF16), vblk, preferred_element_type=F32)
            out.append((m_new, l_new, acc_new))
        return tuple(out)

    init = tuple((jnp.full((tq, 1), -jnp.inf, F32), jnp.zeros((tq, 1), F32),
                  jnp.zeros((tq, head_dim), F32)) for _ in range(group))
    final = lax.fori_loop(0, t // tk, body, init)
    for g in range(group):
        _, l, acc = final[g]
        o_ref[:, g * head_dim:(g + 1) * head_dim] = (acc * (1.0 / l)).astype(o_ref.dtype)


def _flash_call(qkv, prev_out, *, row0, t, n_seq, cfg):
    n = qkv.shape[0]
    hd, group = cfg.head_dim, cfg.n_q_heads // cfg.n_kv_heads
    tq = _tile(t, 256)
    tk = _tile(t, 512)
    assert row0 % t == 0 and row0 % tq == 0
    qb0, sb0 = row0 // tq, row0 // t
    k_col0 = cfg.n_q_heads
    v_col0 = cfg.n_q_heads + cfg.n_kv_heads
    in_specs = [
        pl.BlockSpec((tq, group * hd), lambda b, h, i: (qb0 + b * (t // tq) + i, h)),
        pl.BlockSpec((t, hd), lambda b, h, i: (sb0 + b, k_col0 + h)),
        pl.BlockSpec((t, hd), lambda b, h, i: (sb0 + b, v_col0 + h)),
    ]
    args = [qkv, qkv, qkv]
    aliases = {}
    if prev_out is not None:
        in_specs.append(pl.BlockSpec(memory_space=pl.ANY))
        args.append(prev_out)
        aliases = {3: 0}
    pipelined = 2 * _nbytes((tq, group * hd), BF16) + 2 * _nbytes((t, hd), BF16)
    temps = 6 * group * _nbytes((tq, tk), F32)
    return pl.pallas_call(
        functools.partial(_flash_kernel, tk=tk, group=group, head_dim=hd),
        grid=(n_seq, cfg.n_kv_heads, t // tq),
        in_specs=in_specs,
        out_specs=pl.BlockSpec((tq, group * hd), lambda b, h, i: (qb0 + b * (t // tq) + i, h)),
        out_shape=jax.ShapeDtypeStruct((n, cfg.n_q_heads * hd), BF16),
        input_output_aliases=aliases,
        compiler_params=pltpu.CompilerParams(
            dimension_semantics=("parallel", "parallel", "arbitrary"),
            vmem_limit_bytes=_vmem_limit(pipelined, temps)),
        name=f"flash_t{t}",
    )(*args)


def _flash_bounded_kernel(q_ref, k_ref, vt_ref, *rest, group, head_dim):
    o_ref, l_ref, acc_ref = rest[-3:]
    tq = q_ref.shape[0]
    n_kb, tk = vt_ref.shape[1], vt_ref.shape[3]
    m = group * tq
    qt = jnp.concatenate([jnp.transpose(q_ref[:, g * head_dim:(g + 1) * head_dim].astype(F32)).astype(BF16)
                          for g in range(group)], axis=1)
    l_ref[...] = jnp.zeros_like(l_ref)
    acc_ref[...] = jnp.zeros_like(acc_ref)

    def body(kb, carry):
        kblk = k_ref[pl.ds(pl.multiple_of(kb * tk, tk), tk), :]
        pt = jnp.exp2(jnp.dot(kblk, qt, preferred_element_type=F32))
        l_ref[...] += jnp.sum(pt.reshape(tk // 8, 8, m), axis=0)
        acc_ref[...] += jnp.dot(vt_ref[0, kb], pt.astype(BF16), preferred_element_type=F32)
        return carry

    lax.fori_loop(0, n_kb, body, 0, unroll=min(4, n_kb))
    ot = acc_ref[...] * (1.0 / jnp.sum(l_ref[...], axis=0, keepdims=True))
    for g in range(group):
        o_ref[:, g * head_dim:(g + 1) * head_dim] = jnp.transpose(ot[:, g * tq:(g + 1) * tq]).astype(o_ref.dtype)


def _flash_bounded_call(qkv, vt, prev_out, *, row0, t, n_seq, tk, cfg):
    n = qkv.shape[0]
    hd, group = cfg.head_dim, cfg.n_q_heads // cfg.n_kv_heads
    tq = _tile(t, 512)
    assert row0 % t == 0 and row0 % tq == 0 and t % tk == 0
    qb0, sb0 = row0 // tq, row0 // t
    k_col0 = cfg.n_q_heads
    in_specs = [
        pl.BlockSpec((tq, group * hd), lambda b, h, i: (qb0 + b * (t // tq) + i, h)),
        pl.BlockSpec((t, hd), lambda b, h, i: (sb0 + b, k_col0 + h)),
        pl.BlockSpec((1, t // tk, hd, tk), lambda b, h, i: (h, sb0 + b, 0, 0)),
    ]
    args = [qkv, qkv, vt]
    aliases = {}
    if prev_out is not None:
        in_specs.append(pl.BlockSpec(memory_space=pl.ANY))
        args.append(prev_out)
        aliases = {3: 0}
    m = group * tq
    pipelined = 2 * _nbytes((tq, group * hd), BF16) + 2 * _nbytes((t, hd), BF16)
    resident = _nbytes((8 + hd, m), F32) + 3 * _nbytes((tk, m), F32)
    return pl.pallas_call(
        functools.partial(_flash_bounded_kernel, group=group, head_dim=hd),
        grid=(n_seq, cfg.n_kv_heads, t // tq),
        in_specs=in_specs,
        out_specs=pl.BlockSpec((tq, group * hd), lambda b, h, i: (qb0 + b * (t // tq) + i, h)),
        out_shape=jax.ShapeDtypeStruct((n, cfg.n_q_heads * hd), BF16),
        scratch_shapes=[pltpu.VMEM((8, m), F32), pltpu.VMEM((hd, m), F32)],
        input_output_aliases=aliases,
        compiler_params=pltpu.CompilerParams(
            dimension_semantics=("parallel", "parallel", "arbitrary"),
            vmem_limit_bytes=_vmem_limit(pipelined, resident)),
        name=f"flash_bounded_t{t}",
    )(*args)


SCORE_BOUND_LOG2 = 60.0


def attention(qkv, score_bound, cfg):
    lens = cfg.seq_lens
    runs, row0, idx = [], 0, 0
    while idx < len(lens):
        t, n_seq = lens[idx], 1
        while idx + n_seq < len(lens) and lens[idx + n_seq] == t:
            n_seq += 1
        runs.append((row0, t, n_seq))
        row0 += t * n_seq
        idx += n_seq

    def out_init(qkv):
        if len(runs) == 1:
            return None
        return jnp.zeros((qkv.shape[0], cfg.n_q_heads * cfg.head_dim), BF16)

    def online(qkv):
        out = out_init(qkv)
        for row0, t, n_seq in runs:
            out = _flash_call(qkv, out, row0=row0, t=t, n_seq=n_seq, cfg=cfg)
        return out

    def bounded(qkv):
        n, hd = qkv.shape[0], cfg.head_dim
        tk = _tile(math.gcd(*lens), 1024)
        nq, nkv = cfg.n_q_heads, cfg.n_kv_heads
        vt = qkv[:, (nq + nkv) * hd:].reshape(n // tk, tk, nkv, hd).transpose(2, 0, 3, 1)
        out = out_init(qkv)
        for row0, t, n_seq in runs:
            out = _flash_bounded_call(qkv, vt, out, row0=row0, t=t, n_seq=n_seq, tk=tk, cfg=cfg)
        return out

    return lax.cond(score_bound <= SCORE_BOUND_LOG2, bounded, online, qkv)


def _log_sigmoid(x):
    return jnp.minimum(x, 0.0) - jnp.log(1.0 + jnp.exp(-jnp.abs(x)))


def _gla_block(fwd, q_ref, k_ref, v_ref, a_ref, wa2, ba, tri_ref, o_ref, st_ref, *, chunk, n_chunks, tau, q_scale):
    dk, dv = st_ref.shape
    rblk = n_chunks * chunk
    log_q_scale = math.log(q_scale)
    q, k, v = q_ref[...], k_ref[...], v_ref[...]
    logits = jnp.dot(a_ref[...].astype(BF16), wa2, preferred_element_type=F32)
    g = _log_sigmoid(logits + ba) * (1.0 / tau)

    g_hi = g.astype(BF16)
    rest = g - g_hi.astype(F32)
    g_mid = rest.astype(BF16)
    g_lo = (rest - g_mid.astype(F32)).astype(BF16)
    tri = tri_ref[...]
    prefix = (jnp.dot(tri, g_hi, preferred_element_type=F32) + jnp.dot(tri, g_mid, preferred_element_type=F32)
              + jnp.dot(tri, g_lo, preferred_element_type=F32))
    totals = [prefix[(c + 1) * chunk - 1:(c + 1) * chunk, :] for c in range(n_chunks)]
    before = [sum(totals[:c], jnp.zeros_like(totals[0])) for c in range(n_chunks)]
    total = before[-1] + totals[-1]

    local, ahead, cum = [], [], []
    for c in range(n_chunks):
        rows = slice(c * chunk, (c + 1) * chunk)
        loc = prefix[rows] if fwd else totals[c] - prefix[rows] + g[rows]
        ahd = before[c] if fwd else total - before[c] - totals[c]
        local.append(loc)
        ahead.append(ahd)
        cum.append(loc + ahd)
    cum = jnp.concatenate(cum, axis=0)

    st = st_ref[...]
    q_loc = [q[c * chunk:(c + 1) * chunk] * jnp.exp(local[c] + log_q_scale) for c in range(n_chunks)]
    q_blk = jnp.concatenate([q_loc[c] * jnp.exp(ahead[c]) for c in range(n_chunks)], axis=0).astype(BF16)
    o = jnp.dot(q_blk, st.astype(BF16), preferred_element_type=F32)

    attn = []
    for c in range(n_chunks):
        k_ref_c = (k * jnp.exp(ahead[c] - cum)).astype(BF16)
        attn.append(lax.dot_general(q_loc[c].astype(BF16), k_ref_c, (((1,), (1,)), ((), ())),
                                    preferred_element_type=F32))
    attn = jnp.concatenate(attn, axis=0)
    row = lax.broadcasted_iota(jnp.int32, (rblk, rblk), 0)
    col = lax.broadcasted_iota(jnp.int32, (rblk, rblk), 1)
    mask = (row >= col) if fwd else (col > row)
    o += jnp.dot(jnp.where(mask, attn, 0.0).astype(BF16), v, preferred_element_type=F32)
    o_ref[...] = o

    k_dec = (k * jnp.exp(total - cum)).astype(BF16)
    dec = jnp.transpose(jnp.broadcast_to(jnp.exp(total), (V7X_LANES, dk)))
    dec = jnp.concatenate([dec] * (dv // V7X_LANES), axis=1)
    st_ref[...] = st * dec + lax.dot_general(k_dec, v, (((0,), (0,)), ((), ())), preferred_element_type=F32)


def _gla_kernel(starts_ref, ends_ref, qf_ref, kf_ref, vf_ref, af_ref, qb_ref, kb_ref, vb_ref, ab_ref,
                wa2_ref, ba_ref, tri_ref, of_ref, ob_ref, st_ref, *, n_blocks, **block_args):
    i = pl.program_id(1)

    @pl.when(starts_ref[i] == 1)
    def _():
        st_ref[0] = jnp.zeros(st_ref.shape[1:], st_ref.dtype)

    @pl.when(ends_ref[n_blocks - 1 - i] == 1)
    def _():
        st_ref[1] = jnp.zeros(st_ref.shape[1:], st_ref.dtype)

    _gla_block(True, qf_ref, kf_ref, vf_ref, af_ref, wa2_ref[0], ba_ref[0], tri_ref, of_ref, st_ref.at[0],
               **block_args)
    _gla_block(False, qb_ref, kb_ref, vb_ref, ab_ref, wa2_ref[1], ba_ref[1], tri_ref, ob_ref, st_ref.at[1],
               **block_args)


def gla_scan(qk, v, a, wa2, ba, cfg):
    n = qk.shape[0]
    nh, dk, dv, chunk = cfg.gla_heads, cfg.gla_dk, cfg.gla_dv, cfg.gla_chunk
    rblk = _tile(math.gcd(*cfg.seq_lens), 256)
    n_blocks = n // rblk
    starts, ends, row = [0] * n_blocks, [0] * n_blocks, 0
    for t in cfg.seq_lens:
        starts[row // rblk] = 1
        row += t
        ends[row // rblk - 1] = 1
    starts = jnp.asarray(starts, jnp.int32)
    ends = jnp.asarray(ends, jnp.int32)
    r_idx = jnp.arange(rblk)
    tri = ((r_idx[:, None] // chunk == r_idx[None, :] // chunk) & (r_idx[None, :] <= r_idx[:, None])).astype(BF16)

    def streams(row_block):
        return [pl.BlockSpec((rblk, dk), lambda h, i, s, e: (row_block(i), h)),
                pl.BlockSpec((rblk, dk), lambda h, i, s, e: (row_block(i), nh + h)),
                pl.BlockSpec((rblk, dv), lambda h, i, s, e: (row_block(i), h)),
                pl.BlockSpec((rblk, V7X_LANES), lambda h, i, s, e: (row_block(i), 0))]

    fwd_block = lambda i: i
    bwd_block = lambda i: n_blocks - 1 - i
    grid_spec = pltpu.PrefetchScalarGridSpec(
        num_scalar_prefetch=2,
        grid=(nh, n_blocks),
        in_specs=streams(fwd_block) + streams(bwd_block) + [
            pl.BlockSpec((2, V7X_LANES, dk), lambda h, i, s, e: (0, 0, h)),
            pl.BlockSpec((2, 1, dk), lambda h, i, s, e: (0, 0, h)),
            pl.BlockSpec((rblk, rblk), lambda h, i, s, e: (0, 0)),
        ],
        out_specs=[pl.BlockSpec((rblk, dv), lambda h, i, s, e: (fwd_block(i), h)),
                   pl.BlockSpec((rblk, dv), lambda h, i, s, e: (bwd_block(i), h))],
        scratch_shapes=[pltpu.VMEM((2, dk, dv), F32)],
    )
    pipelined = 2 * (2 * _nbytes((rblk, dk), F32) + _nbytes((rblk, dv), BF16) + _nbytes((rblk, V7X_LANES), F32)
                     + _nbytes((V7X_LANES, dk), BF16) + _nbytes((rblk, dv), F32))
    return pl.pallas_call(
        functools.partial(_gla_kernel, chunk=chunk, n_chunks=rblk // chunk, n_blocks=n_blocks,
                          tau=cfg.gla_tau, q_scale=dk ** -0.5),
        grid_spec=grid_spec,
        out_shape=[jax.ShapeDtypeStruct((n, nh * dv), F32)] * 2,
        compiler_params=pltpu.CompilerParams(
            dimension_semantics=("parallel", "arbitrary"),
            vmem_limit_bytes=_vmem_limit(pipelined, 8 * _nbytes((dv, dk), F32))),
        name="gla_scan",
    )(starts, ends, qk, qk, v, a, qk, qk, v, a, wa2, ba, tri)


def _trunk(xa, xb, p, cfg):
    d = cfg.d_model
    hd = cfg.head_dim
    na, n = xa.shape[0], xa.shape[0] + xb.shape[0]
    bf = lambda w: w.astype(BF16)
    TM = _tile(math.gcd(na, n - na), 1024)
    a_blocks = na // TM

    h = rmsnorm_pair(xa, xb, p["norm_mix"][0], BF16, cfg)
    w_qkv = bf(jnp.concatenate([p["attn_wq"][0], p["attn_wk"][0], p["attn_wv"][0]], axis=1))
    cos, sin_lo, sin_hi = _rope_tables(cfg)
    tn_qkv = math.gcd(4 * hd, cfg.n_kv_heads * hd)
    tab = lambda arr: (arr, (TM if n >= TM else n, hd), lambda i, j: (i, 0))
    vec = lambda arr: (arr.reshape(1, -1), (1, arr.size), lambda i, j: (0, 0))
    q_scale = hd ** -0.5 * math.log2(math.e)
    qkv = matmul(
        h, [w_qkv],
        functools.partial(_epi_qkv, n_q_blocks=cfg.n_q_heads * hd // tn_qkv,
                          n_k_blocks=cfg.n_kv_heads * hd // tn_qkv, head_dim=hd, eps=cfg.eps),
        out_cols=w_qkv.shape[1], out_dtype=BF16, tm=TM, tn=tn_qkv, tk=d,
        extras=[tab(cos), tab(sin_lo), tab(sin_hi),
                vec(p["attn_q_norm"][0] * q_scale), vec(p["attn_k_norm"][0])],
        name="qkv_proj")
    score_bound = (hd * jnp.max(jnp.abs(p["attn_q_norm"][0] * q_scale)) * jnp.max(jnp.abs(p["attn_k_norm"][0]))
                   * (1.0 + 2.0 ** -6))
    o = attention(qkv, score_bound, cfg)
    res = lambda arr, tn: (arr, (TM, tn), lambda i, j: (i, j))
    x = matmul(o, [bf(p["attn_wo"][0])], functools.partial(_epi_residual_pair, a_blocks=a_blocks),
               out_cols=d, out_dtype=F32, tm=TM, tn=512, tk=o.shape[1],
               extras=[(xa, (TM, 512), lambda i, j: (jnp.minimum(i, a_blocks - 1),
                                                     jnp.where(i < a_blocks, j, d // 512 - 1))),
                       (xb, (TM, 512), lambda i, j: (jnp.maximum(i - a_blocks, 0),
                                                     jnp.where(i < a_blocks, 0, j)))],
               name="attn_out")

    h = rmsnorm(x, p["norm_ffn"][0], BF16, cfg)
    hid = matmul(h, [bf(p["ffn_w1"][0]), bf(p["ffn_w3"][0])], _epi_swiglu,
                 out_cols=cfg.d_ff, out_dtype=BF16, tm=TM, tn=512, tk=d, name="ffn_up")
    x = matmul(hid, [bf(p["ffn_w2"][0])], _epi_residual, out_cols=d, out_dtype=F32,
               tm=TM, tn=1024, tk=2048, extras=[res(x, 1024)], name="ffn_down")

    nh, dk, dv, rank = cfg.gla_heads, cfg.gla_dk, cfg.gla_dv, cfg.gla_rank
    h = rmsnorm(x, p["norm_mix"][1], BF16, cfg)
    qk = matmul(h, [bf(jnp.concatenate([p["gla_wq"][0], p["gla_wk"][0]], axis=1))], _epi_store,
                out_cols=2 * nh * dk, out_dtype=F32, tm=TM, tn=1024, tk=d, name="gla_qk")
    v = matmul(h, [bf(p["gla_wv"][0])], _epi_store, out_cols=nh * dv, out_dtype=BF16,
               tm=TM, tn=1024, tk=d, name="gla_v")
    wa1 = jnp.zeros((d, V7X_LANES), F32)
    wa1 = wa1.at[:, :rank].set(p["gla_wa1_f"][0]).at[:, rank:2 * rank].set(p["gla_wa1_b"][0])
    a = matmul(h, [bf(wa1)], _epi_store, out_cols=V7X_LANES, out_dtype=F32,
               tm=TM, tn=V7X_LANES, tk=d, name="gla_gate_lowrank")
    wa2 = jnp.zeros((2, V7X_LANES, nh * dk), F32)
    wa2 = wa2.at[0, :rank].set(p["gla_wa2_f"][0]).at[1, rank:2 * rank].set(p["gla_wa2_b"][0])
    ba = jnp.stack([p["gla_ba_f"][0], p["gla_ba_b"][0]]).reshape(2, 1, nh * dk)
    o_f, o_b = gla_scan(qk, v, a, bf(wa2), ba, cfg)
    gated = matmul(
        h, [bf(p["gla_wg"][0])], functools.partial(_epi_gla_gate, eps=cfg.eps),
        out_cols=nh * dv, out_dtype=BF16, tm=TM // 2, tn=dv, tk=d,
        extras=[(o_f, (TM // 2, dv), lambda i, j: (i, j)), (o_b, (TM // 2, dv), lambda i, j: (i, j)),
                (jnp.tile(p["gla_o_norm"][0], nh).reshape(1, nh * dv), (1, dv), lambda i, j: (0, j))],
        name="gla_gate")
    x = matmul(gated, [bf(p["gla_wo"][0])], _epi_residual, out_cols=d, out_dtype=F32,
               tm=TM, tn=1024, tk=nh * dv, extras=[res(x, 1024)], name="gla_out")

    route = router(x, p["norm_ffn"][1], p["moe_router"][0], cfg)
    return routed_moe_final(x, route, p["norm_ffn"][1], bf(p["moe_w1"][0]), bf(p["moe_w3"][0]),
                            bf(p["moe_w2"][0]), p["norm_final"], na, cfg)


def kernel(x_prompt, x_sample, norm_mix, norm_ffn, norm_final, attn_wq, attn_wk, attn_wv, attn_q_norm, attn_k_norm, attn_wo, gla_wq, gla_wk, gla_wv, gla_wg, gla_wa1_f, gla_wa2_f, gla_ba_f, gla_wa1_b, gla_wa2_b, gla_ba_b, gla_o_norm, gla_wo, ffn_w1, ffn_w3, ffn_w2, moe_router, moe_w1, moe_w3, moe_w2):
    cfg = PROD
    d = cfg.d_model
    params = dict(
        norm_mix=norm_mix, norm_ffn=norm_ffn, norm_final=norm_final,
        attn_wq=attn_wq, attn_wk=attn_wk, attn_wv=attn_wv, attn_q_norm=attn_q_norm,
        attn_k_norm=attn_k_norm, attn_wo=attn_wo,
        gla_wq=gla_wq, gla_wk=gla_wk, gla_wv=gla_wv, gla_wg=gla_wg,
        gla_wa1_f=gla_wa1_f, gla_wa2_f=gla_wa2_f, gla_ba_f=gla_ba_f,
        gla_wa1_b=gla_wa1_b, gla_wa2_b=gla_wa2_b, gla_ba_b=gla_ba_b,
        gla_o_norm=gla_o_norm, gla_wo=gla_wo,
        ffn_w1=ffn_w1, ffn_w3=ffn_w3, ffn_w2=ffn_w2,
        moe_router=moe_router, moe_w1=moe_w1, moe_w3=moe_w3, moe_w2=moe_w2)
    y_prompt, y_sample = _trunk(x_prompt.reshape(-1, d), x_sample.reshape(-1, d), params, cfg)
    return (y_prompt.reshape(x_prompt.shape), y_sample.reshape(x_sample.shape))
```

```python
import functools
import math
from typing import NamedTuple

import jax
import jax.numpy as jnp
from jax import lax
from jax.experimental import pallas as pl
from jax.experimental.pallas import tpu as pltpu

F32 = jnp.float32
BF16 = jnp.bfloat16

V7X_VMEM_BYTES = 64 * 1024 * 1024
V7X_LANES = 128
VMEM_CAP_BYTES = V7X_VMEM_BYTES - 8 * 1024 * 1024


class Cfg(NamedTuple):
    d_model: int
    seq_lens: tuple
    grid_w: int
    head_dim: int
    n_q_heads: int
    n_kv_heads: int
    rope_theta: float
    gla_heads: int
    gla_dk: int
    gla_dv: int
    gla_rank: int
    gla_tau: float
    gla_chunk: int
    d_ff: int
    n_experts: int
    moe_d_ff: int
    eps: float


PROD = Cfg(
    d_model=4096, seq_lens=(16384, 2048, 2048, 2048, 2048), grid_w=64,
    head_dim=128, n_q_heads=32, n_kv_heads=8, rope_theta=10000.0,
    gla_heads=4, gla_dk=512, gla_dv=1024, gla_rank=16, gla_tau=16.0, gla_chunk=64,
    d_ff=8192, n_experts=8, moe_d_ff=1024, eps=1e-6)


def _vmem_limit(pipelined_bytes, resident_bytes):
    need = 2 * pipelined_bytes + resident_bytes
    return int(min(VMEM_CAP_BYTES, max(need, 16 * 1024 * 1024)))


def _nbytes(shape, dtype):
    return math.prod(shape) * jnp.dtype(dtype).itemsize


def _tile(n, pref):
    t = min(n, pref)
    assert n % t == 0, (n, pref)
    return t


def _rms(x, eps):
    return x * lax.rsqrt(jnp.mean(x * x, axis=-1, keepdims=True) + eps)


def _rmsnorm_kernel(x_ref, w_ref, o_ref, *, eps):
    o_ref[...] = (_rms(x_ref[...], eps) * w_ref[...]).astype(o_ref.dtype)


def rmsnorm(x, w, out_dtype, cfg):
    n, d = x.shape
    tm = _tile(n, 256)
    return pl.pallas_call(
        functools.partial(_rmsnorm_kernel, eps=cfg.eps),
        grid=(n // tm,),
        in_specs=[pl.BlockSpec((tm, d), lambda i: (i, 0)),
                  pl.BlockSpec((1, d), lambda i: (0, 0))],
        out_specs=pl.BlockSpec((tm, d), lambda i: (i, 0)),
        out_shape=jax.ShapeDtypeStruct((n, d), out_dtype),
        compiler_params=pltpu.CompilerParams(
            dimension_semantics=("parallel",),
            vmem_limit_bytes=_vmem_limit(_nbytes((tm, d), F32) + _nbytes((tm, d), out_dtype),
                                         2 * _nbytes((tm, d), F32))),
        name="rmsnorm",
    )(x, w.reshape(1, d))


def _rmsnorm_pair_kernel(xa_ref, xb_ref, w_ref, o_ref, *, eps, a_blocks):
    i = pl.program_id(0)

    @pl.when(i < a_blocks)
    def _():
        o_ref[...] = (_rms(xa_ref[...], eps) * w_ref[...]).astype(o_ref.dtype)

    @pl.when(i >= a_blocks)
    def _():
        o_ref[...] = (_rms(xb_ref[...], eps) * w_ref[...]).astype(o_ref.dtype)


def rmsnorm_pair(xa, xb, w, out_dtype, cfg):
    na, nb, d = xa.shape[0], xb.shape[0], xa.shape[1]
    tm = _tile(math.gcd(na, nb), 256)
    a_blocks = na // tm
    return pl.pallas_call(
        functools.partial(_rmsnorm_pair_kernel, eps=cfg.eps, a_blocks=a_blocks),
        grid=((na + nb) // tm,),
        in_specs=[pl.BlockSpec((tm, d), lambda i: (jnp.minimum(i, a_blocks - 1), 0)),
                  pl.BlockSpec((tm, d), lambda i: (jnp.maximum(i - a_blocks, 0), 0)),
                  pl.BlockSpec((1, d), lambda i: (0, 0))],
        out_specs=pl.BlockSpec((tm, d), lambda i: (i, 0)),
        out_shape=jax.ShapeDtypeStruct((na + nb, d), out_dtype),
        compiler_params=pltpu.CompilerParams(
            dimension_semantics=("arbitrary",),
            vmem_limit_bytes=_vmem_limit(2 * _nbytes((tm, d), F32) + _nbytes((tm, d), out_dtype),
                                         2 * _nbytes((tm, d), F32))),
        name="rmsnorm_pair",
    )(xa, xb, w.reshape(1, d))


def _router_kernel(x_ref, w_ref, r_ref, route_ref, *, eps, n_experts):
    h = _rms(x_ref[...], eps) * w_ref[...]
    r = r_ref[...]
    h_hi, r_hi = h.astype(BF16), r.astype(BF16)
    h_lo, r_lo = (h - h_hi.astype(F32)).astype(BF16), (r - r_hi.astype(F32)).astype(BF16)
    logits = (jnp.dot(h_hi, r_hi, preferred_element_type=F32) + jnp.dot(h_lo, r_hi, preferred_element_type=F32)
              + jnp.dot(h_hi, r_lo, preferred_element_type=F32))
    lane = lax.broadcasted_iota(jnp.int32, logits.shape, 1)
    neg = jnp.float32(-jnp.inf)
    logits = jnp.where(lane < n_experts, logits, neg)
    v1 = jnp.max(logits, axis=-1, keepdims=True)
    i1 = jnp.min(jnp.where(logits == v1, lane, V7X_LANES), axis=-1, keepdims=True)
    rest = jnp.where(lane == i1, neg, logits)
    v2 = jnp.max(rest, axis=-1, keepdims=True)
    i2 = jnp.min(jnp.where(rest == v2, lane, V7X_LANES), axis=-1, keepdims=True)
    e2 = jnp.exp(v2 - v1)
    denom = 1.0 + e2
    route = jnp.where(lane == 0, i1.astype(F32), 0.0) + jnp.where(lane == 1, i2.astype(F32), 0.0)
    route_ref[...] = route + jnp.where(lane == 2, 1.0 / denom, 0.0) + jnp.where(lane == 3, e2 / denom, 0.0)


def router(x, w, router_w, cfg):
    n, d = x.shape
    tm = _tile(n, 256)
    r_pad = jnp.zeros((d, V7X_LANES), F32).at[:, :cfg.n_experts].set(router_w)
    return pl.pallas_call(
        functools.partial(_router_kernel, eps=cfg.eps, n_experts=cfg.n_experts),
        grid=(n // tm,),
        in_specs=[pl.BlockSpec((tm, d), lambda i: (i, 0)),
                  pl.BlockSpec((1, d), lambda i: (0, 0)),
                  pl.BlockSpec((d, V7X_LANES), lambda i: (0, 0))],
        out_specs=pl.BlockSpec((tm, V7X_LANES), lambda i: (i, 0)),
        out_shape=jax.ShapeDtypeStruct((n, V7X_LANES), F32),
        compiler_params=pltpu.CompilerParams(
            dimension_semantics=("parallel",),
            vmem_limit_bytes=_vmem_limit(_nbytes((tm, d), F32) + _nbytes((d, V7X_LANES), F32),
                                         4 * _nbytes((tm, d), F32))),
        name="router",
    )(x, w.reshape(1, d), r_pad)


def _row_copy(src_hbm, dst_vmem, sem, src_row, dst_row):
    return pltpu.make_async_copy(src_hbm.at[pl.ds(src_row, 1), :], dst_vmem.at[pl.ds(dst_row, 1), :], sem)


def _gather_rows(src_hbm, dst_vmem, sem, row_of, n_rows):
    def start(pair, c):
        for prio in range(2):
            r = 2 * pair + prio
            _row_copy(src_hbm, dst_vmem, sem, row_of(r), r).start(priority=prio)
        return c

    lax.fori_loop(0, n_rows // 2, start, 0, unroll=4)


def _wait_rows(src_hbm, dst_vmem, sem, n_rows):
    def wait(r, c):
        _row_copy(src_hbm, dst_vmem, sem, 0, r).wait()
        return c

    lax.fori_loop(0, n_rows, wait, 0, unroll=8)


def _moe_up_kernel(tile_expert_ref, tile_src_ref, tok_ref, x_hbm, nw_ref, w1_ref, w3_ref, hid_ref,
                   xbuf, xn_ref, sem, *, tm, n_tiles, eps):
    i = pl.program_id(0)
    cur = lax.rem(i, 2)
    nxt_tile = jnp.minimum(i + 1, n_tiles - 1)
    half = hid_ref.shape[1] // 2

    def gather(tile, buf):
        src0 = tile_src_ref[tile]
        _gather_rows(x_hbm, xbuf.at[buf], sem.at[buf], lambda r: tok_ref[src0 + r], tm)

    def normalise(buf):
        xn_ref[buf] = (_rms(xbuf[buf], eps) * nw_ref[...]).astype(xn_ref.dtype)

    @pl.when(i == 0)
    def _():
        gather(0, 0)
        _wait_rows(x_hbm, xbuf.at[0], sem.at[0], tm)
        normalise(0)

    def up_half(c):
        xn = xn_ref[cur]
        cols = slice(c * half, (c + 1) * half)
        a1 = jnp.dot(xn, w1_ref[0, :, cols], preferred_element_type=F32)
        a3 = jnp.dot(xn, w3_ref[0, :, cols], preferred_element_type=F32)
        hid_ref[:, cols] = (_silu(a1) * a3).astype(hid_ref.dtype)

    gather(nxt_tile, 1 - cur)
    up_half(0)
    _wait_rows(x_hbm, xbuf.at[1 - cur], sem.at[1 - cur], tm)
    up_half(1)
    normalise(1 - cur)


def _moe_down_kernel(tile_expert_ref, hid_ref, w2_ref, y_ref):
    y_ref[...] = jnp.dot(hid_ref[...], w2_ref[0], preferred_element_type=F32)


def _moe_combine_kernel(pos_ref, x_ref, route_ref, nw_ref, y_hbm, oa_ref, ob_ref, ybuf, sem,
                        *, tb, n_tokens, n_steps, a_steps, top_k, eps):
    i = pl.program_id(0)
    cur = lax.rem(i, 2)

    def gather(step, buf):
        base = step * tb
        for s in range(top_k):
            _gather_rows(y_hbm, ybuf.at[buf, s], sem.at[buf], lambda r, s=s: pos_ref[s * n_tokens + base + r], tb)

    @pl.when(i == 0)
    def _():
        gather(0, 0)

    @pl.when(i + 1 < n_steps)
    def _():
        gather(i + 1, 1 - cur)

    for s in range(top_k):
        _wait_rows(y_hbm, ybuf.at[cur, s], sem.at[cur], tb)
    acc = x_ref[...]
    route = route_ref[...]
    for s in range(top_k):
        acc = acc + ybuf[cur, s] * route[:, top_k + s:top_k + s + 1]
    out = _rms(acc, eps) * nw_ref[...]

    @pl.when(i < a_steps)
    def _():
        oa_ref[...] = out

    @pl.when(i >= a_steps)
    def _():
        ob_ref[...] = out


def _route_plan(route, tm, cfg):
    n, ne, top_k = route.shape[0], cfg.n_experts, 2
    pairs = top_k * n
    expert = route[:, :top_k].astype(jnp.int32).T.reshape(-1)
    token = jnp.tile(jnp.arange(n, dtype=jnp.int32), top_k)
    index = jnp.arange(pairs, dtype=jnp.int32)
    sorted_expert, sorted_index, sorted_token = lax.sort((expert, index, token), num_keys=1)
    onehot = sorted_expert[:, None] == jnp.arange(ne, dtype=jnp.int32)[None, :]
    counts = jnp.sum(onehot, axis=0).astype(jnp.int32)
    padded = ((counts + tm - 1) // tm) * tm
    pad_end = jnp.cumsum(padded)
    shift = (pad_end - padded) - (jnp.cumsum(counts) - counts)
    dest_sorted = index + jnp.sum(jnp.where(onehot, shift[None, :], 0), axis=1)
    _, pos = lax.sort((sorted_index, dest_sorted), num_keys=1)
    n_tiles = (pairs + ne * tm) // tm
    tile_start = jnp.arange(n_tiles, dtype=jnp.int32) * tm
    tile_expert = jnp.minimum(jnp.sum(tile_start[:, None] >= pad_end[None, :], axis=1), ne - 1).astype(jnp.int32)
    tile_shift = jnp.sum(jnp.where(tile_expert[:, None] == jnp.arange(ne)[None, :], shift[None, :], 0), axis=1)
    tile_src = jnp.clip(tile_start - tile_shift, 0, pairs - 1).astype(jnp.int32)
    tok = jnp.concatenate([sorted_token, jnp.zeros((tm,), jnp.int32)])
    return tile_expert, tile_src, tok, pos


def routed_moe_final(x, route, norm_w, w1, w3, w2, final_w, n_first, cfg):
    n, d = x.shape
    ne, eff, top_k = cfg.n_experts, cfg.moe_d_ff, 2
    tm = _tile(n, 256)
    tile_expert, tile_src, tok, pos = _route_plan(route, tm, cfg)
    n_tiles = tile_expert.shape[0]
    p_rows = n_tiles * tm

    hid = pl.pallas_call(
        functools.partial(_moe_up_kernel, tm=tm, n_tiles=n_tiles, eps=cfg.eps),
        grid_spec=pltpu.PrefetchScalarGridSpec(
            num_scalar_prefetch=3,
            grid=(n_tiles,),
            in_specs=[
                pl.BlockSpec(memory_space=pl.ANY),
                pl.BlockSpec((1, d), lambda i, te, ts, tk: (0, 0)),
                pl.BlockSpec((1, d, eff), lambda i, te, ts, tk: (te[i], 0, 0)),
                pl.BlockSpec((1, d, eff), lambda i, te, ts, tk: (te[i], 0, 0)),
            ],
            out_specs=pl.BlockSpec((tm, eff), lambda i, te, ts, tk: (i, 0)),
            scratch_shapes=[pltpu.VMEM((2, tm, d), F32), pltpu.VMEM((2, tm, d), BF16),
                            pltpu.SemaphoreType.DMA((2,))],
        ),
        out_shape=jax.ShapeDtypeStruct((p_rows, eff), BF16),
        compiler_params=pltpu.CompilerParams(
            dimension_semantics=("arbitrary",),
            vmem_limit_bytes=_vmem_limit(
                2 * _nbytes((d, eff), BF16) + _nbytes((tm, eff), BF16),
                3 * _nbytes((tm, d), F32) + 2 * _nbytes((tm, d), BF16) + 3 * _nbytes((tm, eff), F32))),
        name="moe_up",
    )(tile_expert, tile_src, tok, x, norm_w.reshape(1, d), w1, w3)

    tn2 = _tile(d, 2048)
    y = pl.pallas_call(
        _moe_down_kernel,
        grid_spec=pltpu.PrefetchScalarGridSpec(
            num_scalar_prefetch=1,
            grid=(d // tn2, n_tiles),
            in_specs=[
                pl.BlockSpec((tm, eff), lambda j, i, te: (i, 0)),
                pl.BlockSpec((1, eff, tn2), lambda j, i, te: (te[i], 0, j)),
            ],
            out_specs=pl.BlockSpec((tm, tn2), lambda j, i, te: (i, j)),
        ),
        out_shape=jax.ShapeDtypeStruct((p_rows, d), F32),
        compiler_params=pltpu.CompilerParams(
            dimension_semantics=("parallel", "parallel"),
            vmem_limit_bytes=_vmem_limit(
                _nbytes((tm, eff), BF16) + _nbytes((eff, tn2), BF16) + _nbytes((tm, tn2), F32),
                2 * _nbytes((tm, tn2), F32))),
        name="moe_down",
    )(tile_expert, hid, w2)

    tb = _tile(math.gcd(n_first, n - n_first), 256)
    n_steps, a_steps = n // tb, n_first // tb
    return pl.pallas_call(
        functools.partial(_moe_combine_kernel, tb=tb, n_tokens=n, n_steps=n_steps, a_steps=a_steps,
                          top_k=top_k, eps=cfg.eps),
        grid_spec=pltpu.PrefetchScalarGridSpec(
            num_scalar_prefetch=1,
            grid=(n_steps,),
            in_specs=[
                pl.BlockSpec((tb, d), lambda i, ps: (i, 0)),
                pl.BlockSpec((tb, V7X_LANES), lambda i, ps: (i, 0)),
                pl.BlockSpec((1, d), lambda i, ps: (0, 0)),
                pl.BlockSpec(memory_space=pl.ANY),
            ],
            out_specs=[pl.BlockSpec((tb, d), lambda i, ps: (jnp.minimum(i, a_steps - 1), 0)),
                       pl.BlockSpec((tb, d), lambda i, ps: (jnp.maximum(i - a_steps, 0), 0))],
            scratch_shapes=[pltpu.VMEM((2, top_k, tb, d), F32), pltpu.SemaphoreType.DMA((2,))],
        ),
        out_shape=[jax.ShapeDtypeStruct((n_first, d), F32), jax.ShapeDtypeStruct((n - n_first, d), F32)],
        compiler_params=pltpu.CompilerParams(
            dimension_semantics=("arbitrary",),
            vmem_limit_bytes=_vmem_limit(3 * _nbytes((tb, d), F32), (2 * top_k + 3) * _nbytes((tb, d), F32))),
        name="moe_combine_final_norm",
    )(pos, x, route, final_w.reshape(1, d), y)


def _mm_kernel(*refs, n_w, n_extra, nk, epilogue):
    a_ref = refs[0]
    w_refs = refs[1:1 + n_w]
    extra_refs = refs[1 + n_w:1 + n_w + n_extra]
    o_ref = refs[1 + n_w + n_extra]
    acc_refs = refs[2 + n_w + n_extra:]
    a = a_ref[...]
    parts = [jnp.dot(a, w[...], preferred_element_type=F32) for w in w_refs]
    if nk == 1:
        epilogue(parts, extra_refs, o_ref)
        return
    k = pl.program_id(2)

    @pl.when(k == 0)
    def _():
        for acc, p in zip(acc_refs, parts):
            acc[...] = p

    @pl.when(jnp.logical_and(k > 0, k < nk - 1))
    def _():
        for acc, p in zip(acc_refs, parts):
            acc[...] += p

    @pl.when(k == nk - 1)
    def _():
        epilogue([acc[...] + p for acc, p in zip(acc_refs, parts)], extra_refs, o_ref)


def matmul(a, ws, epilogue, *, out_cols, out_dtype, tm, tn, tk, extras=(), name):
    m, kdim = a.shape
    n = ws[0].shape[1]
    tm, tn, tk = _tile(m, tm), _tile(n, tn), _tile(kdim, tk)
    nk = kdim // tk
    assert out_cols == n
    in_specs = [pl.BlockSpec((tm, tk), lambda i, j, k: (i, k))]
    in_specs += [pl.BlockSpec((tk, tn), lambda i, j, k: (k, j)) for _ in ws]
    pipelined = _nbytes((tm, tk), a.dtype) + len(ws) * _nbytes((tk, tn), ws[0].dtype)
    for arr, blk, imap in extras:
        in_specs.append(pl.BlockSpec(blk, lambda i, j, k, imap=imap: imap(i, j)))
        pipelined += _nbytes(blk, arr.dtype)
    pipelined += _nbytes((tm, tn), out_dtype)
    acc_bytes = len(ws) * _nbytes((tm, tn), F32)
    scratch = [pltpu.VMEM((tm, tn), F32) for _ in ws] if nk > 1 else []
    return pl.pallas_call(
        functools.partial(_mm_kernel, n_w=len(ws), n_extra=len(extras), nk=nk, epilogue=epilogue),
        grid=(m // tm, n // tn, nk),
        in_specs=in_specs,
        out_specs=pl.BlockSpec((tm, tn), lambda i, j, k: (i, j)),
        out_shape=jax.ShapeDtypeStruct((m, out_cols), out_dtype),
        scratch_shapes=scratch,
        compiler_params=pltpu.CompilerParams(
            dimension_semantics=("parallel", "parallel", "arbitrary"),
            vmem_limit_bytes=_vmem_limit(pipelined, 3 * acc_bytes)),
        name=name,
    )(a, *ws, *[e[0] for e in extras])


def _epi_store(parts, extra_refs, o_ref):
    o_ref[...] = parts[0].astype(o_ref.dtype)


def _epi_residual(parts, extra_refs, o_ref):
    o_ref[...] = (extra_refs[0][...] + parts[0]).astype(o_ref.dtype)


def _epi_residual_pair(parts, extra_refs, o_ref, *, a_blocks):
    i = pl.program_id(0)

    @pl.when(i < a_blocks)
    def _():
        o_ref[...] = (extra_refs[0][...] + parts[0]).astype(o_ref.dtype)

    @pl.when(i >= a_blocks)
    def _():
        o_ref[...] = (extra_refs[1][...] + parts[0]).astype(o_ref.dtype)


def _silu(x):
    return x * (1.0 / (1.0 + jnp.exp(-x)))


def _epi_swiglu(parts, extra_refs, o_ref):
    o_ref[...] = (_silu(parts[0]) * parts[1]).astype(o_ref.dtype)


def _epi_qkv(parts, extra_refs, o_ref, *, n_q_blocks, n_k_blocks, head_dim, eps):
    acc = parts[0]
    cos_ref, sin_lo_ref, sin_hi_ref, qw_ref, kw_ref = extra_refs
    j = pl.program_id(1)
    heads = acc.shape[1] // head_dim

    def norm_rope(w):
        cos, sin_lo, sin_hi = cos_ref[...], sin_lo_ref[...], sin_hi_ref[...]
        for g in range(heads):
            sl = slice(g * head_dim, (g + 1) * head_dim)
            y = _rms(acc[:, sl], eps) * w
            r = (y * cos + pltpu.roll(y, head_dim - head_dim // 4, 1) * sin_lo
                 + pltpu.roll(y, head_dim // 4, 1) * sin_hi)
            o_ref[:, sl] = r.astype(o_ref.dtype)

    @pl.when(j < n_q_blocks)
    def _():
        norm_rope(qw_ref[...])

    @pl.when(jnp.logical_and(j >= n_q_blocks, j < n_q_blocks + n_k_blocks))
    def _():
        norm_rope(kw_ref[...])

    @pl.when(j >= n_q_blocks + n_k_blocks)
    def _():
        o_ref[...] = acc.astype(o_ref.dtype)


def _epi_gla_gate(parts, extra_refs, o_ref, *, eps):
    of_ref, ob_ref, w_ref = extra_refs
    o = of_ref[...] + ob_ref[...]
    o_ref[...] = (_rms(o, eps) * w_ref[...] * _silu(parts[0])).astype(o_ref.dtype)


def _rope_tables(cfg):
    hd = cfg.head_dim
    axis_dim = hd // 2
    inv_freq = cfg.rope_theta ** (-jnp.arange(0, axis_dim, 2, dtype=F32) / axis_dim)
    tabs = []
    for t in cfg.seq_lens:
        pos = jnp.arange(t)
        ang_r = (pos // cfg.grid_w).astype(F32)[:, None] * inv_freq
        ang_c = (pos % cfg.grid_w).astype(F32)[:, None] * inv_freq
        tabs.append(jnp.concatenate([ang_r, ang_r, ang_c, ang_c], axis=-1))
    ang = jnp.concatenate(tabs, axis=0)
    cos, sin = jnp.cos(ang), jnp.sin(ang)
    first_half = (jnp.arange(hd) % (hd // 2)) < (hd // 4)
    sin_lo = jnp.where(first_half, -sin, 0.0)
    sin_hi = jnp.where(first_half, 0.0, sin)
    return cos, sin_lo, sin_hi


def _flash_kernel(q_ref, k_ref, v_ref, *rest, tk, group, head_dim):
    o_ref = rest[-1]
    t = k_ref.shape[0]
    tq = q_ref.shape[0]
    qs = [q_ref[:, g * head_dim:(g + 1) * head_dim] for g in range(group)]

    def body(kb, carry):
        off = pl.multiple_of(kb * tk, tk)
        kblk = k_ref[pl.ds(off, tk), :]
        vblk = v_ref[pl.ds(off, tk), :]
        out = []
        for g in range(group):
            m_prev, l_prev, acc_prev = carry[g]
            s = lax.dot_general(qs[g], kblk, (((1,), (1,)), ((), ())), preferred_element_type=F32)
            m_new = jnp.maximum(m_prev, jnp.max(s, axis=-1, keepdims=True))
            p = jnp.exp2(s - m_new)
            alpha = jnp.exp2(m_prev - m_new)
            l_new = alpha * l_prev + jnp.sum(p, axis=-1, keepdims=True)
            acc_new = alpha * acc_prev + jnp.dot(p.astype(BF16), vblk, preferred_element_type=F32)
            out.append((m_new, l_new, acc_new))
        return tuple(out)

    init = tuple((jnp.full((tq, 1), -jnp.inf, F32), jnp.zeros((tq, 1), F32),
                  jnp.zeros((tq, head_dim), F32)) for _ in range(group))
    final = lax.fori_loop(0, t // tk, body, init)
    for g in range(group):
        _, l, acc = final[g]
        o_ref[:, g * head_dim:(g + 1) * head_dim] = (acc * (1.0 / l)).astype(o_ref.dtype)


def _flash_call(qkv, prev_out, *, row0, t, n_seq, cfg):
    n = qkv.shape[0]
    hd, group = cfg.head_dim, cfg.n_q_heads // cfg.n_kv_heads
    tq = _tile(t, 256)
    tk = _tile(t, 512)
    assert row0 % t == 0 and row0 % tq == 0
    qb0, sb0 = row0 // tq, row0 // t
    k_col0 = cfg.n_q_heads
    v_col0 = cfg.n_q_heads + cfg.n_kv_heads
    in_specs = [
        pl.BlockSpec((tq, group * hd), lambda b, h, i: (qb0 + b * (t // tq) + i, h)),
        pl.BlockSpec((t, hd), lambda b, h, i: (sb0 + b, k_col0 + h)),
        pl.BlockSpec((t, hd), lambda b, h, i: (sb0 + b, v_col0 + h)),
    ]
    args = [qkv, qkv, qkv]
    aliases = {}
    if prev_out is not None:
        in_specs.append(pl.BlockSpec(memory_space=pl.ANY))
        args.append(prev_out)
        aliases = {3: 0}
    pipelined = 2 * _nbytes((tq, group * hd), BF16) + 2 * _nbytes((t, hd), BF16)
    temps = 6 * group * _nbytes((tq, tk), F32)
    return pl.pallas_call(
        functools.partial(_flash_kernel, tk=tk, group=group, head_dim=hd),
        grid=(n_seq, cfg.n_kv_heads, t // tq),
        in_specs=in_specs,
        out_specs=pl.BlockSpec((tq, group * hd), lambda b, h, i: (qb0 + b * (t // tq) + i, h)),
        out_shape=jax.ShapeDtypeStruct((n, cfg.n_q_heads * hd), BF16),
        input_output_aliases=aliases,
        compiler_params=pltpu.CompilerParams(
            dimension_semantics=("parallel", "parallel", "arbitrary"),
            vmem_limit_bytes=_vmem_limit(pipelined, temps)),
        name=f"flash_t{t}",
    )(*args)


def _flash_bounded_kernel(q_ref, k_ref, vt_ref, *rest, group, head_dim):
    o_ref, l_ref, acc_ref = rest[-3:]
    tq = q_ref.shape[0]
    n_kb, tk = vt_ref.shape[1], vt_ref.shape[3]
    m = group * tq
    qt = jnp.concatenate([jnp.transpose(q_ref[:, g * head_dim:(g + 1) * head_dim].astype(F32)).astype(BF16)
                          for g in range(group)], axis=1)
    l_ref[...] = jnp.zeros_like(l_ref)
    acc_ref[...] = jnp.zeros_like(acc_ref)

    def body(kb, carry):
        kblk = k_ref[pl.ds(pl.multiple_of(kb * tk, tk), tk), :]
        pt = jnp.exp2(jnp.dot(kblk, qt, preferred_element_type=F32))
        l_ref[...] += jnp.sum(pt.reshape(tk // 8, 8, m), axis=0)
        acc_ref[...] += jnp.dot(vt_ref[0, kb], pt.astype(BF16), preferred_element_type=F32)
        return carry

    lax.fori_loop(0, n_kb, body, 0, unroll=min(4, n_kb))
    ot = acc_ref[...] * (1.0 / jnp.sum(l_ref[...], axis=0, keepdims=True))
    for g in range(group):
        o_ref[:, g * head_dim:(g + 1) * head_dim] = jnp.transpose(ot[:, g * tq:(g + 1) * tq]).astype(o_ref.dtype)


def _flash_bounded_call(qkv, vt, prev_out, *, row0, t, n_seq, tk, cfg):
    n = qkv.shape[0]
    hd, group = cfg.head_dim, cfg.n_q_heads // cfg.n_kv_heads
    tq = _tile(t, 512)
    assert row0 % t == 0 and row0 % tq == 0 and t % tk == 0
    qb0, sb0 = row0 // tq, row0 // t
    k_col0 = cfg.n_q_heads
    in_specs = [
        pl.BlockSpec((tq, group * hd), lambda b, h, i: (qb0 + b * (t // tq) + i, h)),
        pl.BlockSpec((t, hd), lambda b, h, i: (sb0 + b, k_col0 + h)),
        pl.BlockSpec((1, t // tk, hd, tk), lambda b, h, i: (h, sb0 + b, 0, 0)),
    ]
    args = [qkv, qkv, vt]
    aliases = {}
    if prev_out is not None:
        in_specs.append(pl.BlockSpec(memory_space=pl.ANY))
        args.append(prev_out)
        aliases = {3: 0}
    m = group * tq
    pipelined = 2 * _nbytes((tq, group * hd), BF16) + 2 * _nbytes((t, hd), BF16)
    resident = _nbytes((8 + hd, m), F32) + 3 * _nbytes((tk, m), F32)
    return pl.pallas_call(
        functools.partial(_flash_bounded_kernel, group=group, head_dim=hd),
        grid=(n_seq, cfg.n_kv_heads, t // tq),
        in_specs=in_specs,
        out_specs=pl.BlockSpec((tq, group * hd), lambda b, h, i: (qb0 + b * (t // tq) + i, h)),
        out_shape=jax.ShapeDtypeStruct((n, cfg.n_q_heads * hd), BF16),
        scratch_shapes=[pltpu.VMEM((8, m), F32), pltpu.VMEM((hd, m), F32)],
        input_output_aliases=aliases,
        compiler_params=pltpu.CompilerParams(
            dimension_semantics=("parallel", "parallel", "arbitrary"),
            vmem_limit_bytes=_vmem_limit(pipelined, resident)),
        name=f"flash_bounded_t{t}",
    )(*args)


SCORE_BOUND_LOG2 = 60.0


def attention(qkv, score_bound, cfg):
    lens = cfg.seq_lens
    runs, row0, idx = [], 0, 0
    while idx < len(lens):
        t, n_seq = lens[idx], 1
        while idx + n_seq < len(lens) and lens[idx + n_seq] == t:
            n_seq += 1
        runs.append((row0, t, n_seq))
        row0 += t * n_seq
        idx += n_seq

    def out_init(qkv):
        if len(runs) == 1:
            return None
        return jnp.zeros((qkv.shape[0], cfg.n_q_heads * cfg.head_dim), BF16)

    def online(qkv):
        out = out_init(qkv)
        for row0, t, n_seq in runs:
            out = _flash_call(qkv, out, row0=row0, t=t, n_seq=n_seq, cfg=cfg)
        return out

    def bounded(qkv):
        n, hd = qkv.shape[0], cfg.head_dim
        tk = _tile(math.gcd(*lens), 1024)
        nq, nkv = cfg.n_q_heads, cfg.n_kv_heads
        vt = qkv[:, (nq + nkv) * hd:].reshape(n // tk, tk, nkv, hd).transpose(2, 0, 3, 1)
        out = out_init(qkv)
        for row0, t, n_seq in runs:
            out = _flash_bounded_call(qkv, vt, out, row0=row0, t=t, n_seq=n_seq, tk=tk, cfg=cfg)
        return out

    return lax.cond(score_bound <= SCORE_BOUND_LOG2, bounded, online, qkv)


def _log_sigmoid(x):
    return jnp.minimum(x, 0.0) - jnp.log(1.0 + jnp.exp(-jnp.abs(x)))


def _gla_block(fwd, q_ref, k_ref, v_ref, a_ref, wa2, ba, tri_ref, o_ref, st_ref, *, chunk, n_chunks, tau, q_scale):
    dk, dv = st_ref.shape
    rblk = n_chunks * chunk
    log_q_scale = math.log(q_scale)
    q, k, v = q_ref[...], k_ref[...], v_ref[...]
    logits = jnp.dot(a_ref[...].astype(BF16), wa2, preferred_element_type=F32)
    g = _log_sigmoid(logits + ba) * (1.0 / tau)

    g_hi = g.astype(BF16)
    rest = g - g_hi.astype(F32)
    g_mid = rest.astype(BF16)
    g_lo = (rest - g_mid.astype(F32)).astype(BF16)
    tri = tri_ref[...]
    prefix = (jnp.dot(tri, g_hi, preferred_element_type=F32) + jnp.dot(tri, g_mid, preferred_element_type=F32)
              + jnp.dot(tri, g_lo, preferred_element_type=F32))
    totals = [prefix[(c + 1) * chunk - 1:(c + 1) * chunk, :] for c in range(n_chunks)]
    before = [sum(totals[:c], jnp.zeros_like(totals[0])) for c in range(n_chunks)]
    total = before[-1] + totals[-1]

    local, ahead, cum = [], [], []
    for c in range(n_chunks):
        rows = slice(c * chunk, (c + 1) * chunk)
        loc = prefix[rows] if fwd else totals[c] - prefix[rows] + g[rows]
        ahd = before[c] if fwd else total - before[c] - totals[c]
        local.append(loc)
        ahead.append(ahd)
        cum.append(loc + ahd)
    cum = jnp.concatenate(cum, axis=0)

    st = st_ref[...]
    q_loc = [q[c * chunk:(c + 1) * chunk] * jnp.exp(local[c] + log_q_scale) for c in range(n_chunks)]
    q_blk = jnp.concatenate([q_loc[c] * jnp.exp(ahead[c]) for c in range(n_chunks)], axis=0).astype(BF16)
    o = jnp.dot(q_blk, st.astype(BF16), preferred_element_type=F32)

    attn = []
    for c in range(n_chunks):
        k_ref_c = (k * jnp.exp(ahead[c] - cum)).astype(BF16)
        attn.append(lax.dot_general(q_loc[c].astype(BF16), k_ref_c, (((1,), (1,)), ((), ())),
                                    preferred_element_type=F32))
    attn = jnp.concatenate(attn, axis=0)
    row = lax.broadcasted_iota(jnp.int32, (rblk, rblk), 0)
    col = lax.broadcasted_iota(jnp.int32, (rblk, rblk), 1)
    mask = (row >= col) if fwd else (col > row)
    o += jnp.dot(jnp.where(mask, attn, 0.0).astype(BF16), v, preferred_element_type=F32)
    o_ref[...] = o

    k_dec = (k * jnp.exp(total - cum)).astype(BF16)
    dec = jnp.transpose(jnp.broadcast_to(jnp.exp(total), (V7X_LANES, dk)))
    dec = jnp.concatenate([dec] * (dv // V7X_LANES), axis=1)
    st_ref[...] = st * dec + lax.dot_general(k_dec, v, (((0,), (0,)), ((), ())), preferred_element_type=F32)


def _gla_kernel(starts_ref, ends_ref, qf_ref, kf_ref, vf_ref, af_ref, qb_ref, kb_ref, vb_ref, ab_ref,
                wa2_ref, ba_ref, tri_ref, of_ref, ob_ref, st_ref, *, n_blocks, **block_args):
    i = pl.program_id(1)

    @pl.when(starts_ref[i] == 1)
    def _():
        st_ref[0] = jnp.zeros(st_ref.shape[1:], st_ref.dtype)

    @pl.when(ends_ref[n_blocks - 1 - i] == 1)
    def _():
        st_ref[1] = jnp.zeros(st_ref.shape[1:], st_ref.dtype)

    _gla_block(True, qf_ref, kf_ref, vf_ref, af_ref, wa2_ref[0], ba_ref[0], tri_ref, of_ref, st_ref.at[0],
               **block_args)
    _gla_block(False, qb_ref, kb_ref, vb_ref, ab_ref, wa2_ref[1], ba_ref[1], tri_ref, ob_ref, st_ref.at[1],
               **block_args)


def gla_scan(qk, v, a, wa2, ba, cfg):
    n = qk.shape[0]
    nh, dk, dv, chunk = cfg.gla_heads, cfg.gla_dk, cfg.gla_dv, cfg.gla_chunk
    rblk = _tile(math.gcd(*cfg.seq_lens), 256)
    n_blocks = n // rblk
    starts, ends, row = [0] * n_blocks, [0] * n_blocks, 0
    for t in cfg.seq_lens:
        starts[row // rblk] = 1
        row += t
        ends[row // rblk - 1] = 1
    starts = jnp.asarray(starts, jnp.int32)
    ends = jnp.asarray(ends, jnp.int32)
    r_idx = jnp.arange(rblk)
    tri = ((r_idx[:, None] // chunk == r_idx[None, :] // chunk) & (r_idx[None, :] <= r_idx[:, None])).astype(BF16)

    def streams(row_block):
        return [pl.BlockSpec((rblk, dk), lambda h, i, s, e: (row_block(i), h)),
                pl.BlockSpec((rblk, dk), lambda h, i, s, e: (row_block(i), nh + h)),
                pl.BlockSpec((rblk, dv), lambda h, i, s, e: (row_block(i), h)),
                pl.BlockSpec((rblk, V7X_LANES), lambda h, i, s, e: (row_block(i), 0))]

    fwd_block = lambda i: i
    bwd_block = lambda i: n_blocks - 1 - i
    grid_spec = pltpu.PrefetchScalarGridSpec(
        num_scalar_prefetch=2,
        grid=(nh, n_blocks),
        in_specs=streams(fwd_block) + streams(bwd_block) + [
            pl.BlockSpec((2, V7X_LANES, dk), lambda h, i, s, e: (0, 0, h)),
            pl.BlockSpec((2, 1, dk), lambda h, i, s, e: (0, 0, h)),
            pl.BlockSpec((rblk, rblk), lambda h, i, s, e: (0, 0)),
        ],
        out_specs=[pl.BlockSpec((rblk, dv), lambda h, i, s, e: (fwd_block(i), h)),
                   pl.BlockSpec((rblk, dv), lambda h, i, s, e: (bwd_block(i), h))],
        scratch_shapes=[pltpu.VMEM((2, dk, dv), F32)],
    )
    pipelined = 2 * (2 * _nbytes((rblk, dk), F32) + _nbytes((rblk, dv), BF16) + _nbytes((rblk, V7X_LANES), F32)
                     + _nbytes((V7X_LANES, dk), BF16) + _nbytes((rblk, dv), F32))
    return pl.pallas_call(
        functools.partial(_gla_kernel, chunk=chunk, n_chunks=rblk // chunk, n_blocks=n_blocks,
                          tau=cfg.gla_tau, q_scale=dk ** -0.5),
        grid_spec=grid_spec,
        out_shape=[jax.ShapeDtypeStruct((n, nh * dv), F32)] * 2,
        compiler_params=pltpu.CompilerParams(
            dimension_semantics=("parallel", "arbitrary"),
            vmem_limit_bytes=_vmem_limit(pipelined, 8 * _nbytes((dv, dk), F32))),
        name="gla_scan",
    )(starts, ends, qk, qk, v, a, qk, qk, v, a, wa2, ba, tri)


def _trunk(xa, xb, p, cfg):
    d = cfg.d_model
    hd = cfg.head_dim
    na, n = xa.shape[0], xa.shape[0] + xb.shape[0]
    bf = lambda w: w.astype(BF16)
    TM = _tile(math.gcd(na, n - na), 1024)
    a_blocks = na // TM

    h = rmsnorm_pair(xa, xb, p["norm_mix"][0], BF16, cfg)
    w_qkv = bf(jnp.concatenate([p["attn_wq"][0], p["attn_wk"][0], p["attn_wv"][0]], axis=1))
    cos, sin_lo, sin_hi = _rope_tables(cfg)
    tn_qkv = math.gcd(4 * hd, cfg.n_kv_heads * hd)
    tab = lambda arr: (arr, (TM if n >= TM else n, hd), lambda i, j: (i, 0))
    vec = lambda arr: (arr.reshape(1, -1), (1, arr.size), lambda i, j: (0, 0))
    q_scale = hd ** -0.5 * math.log2(math.e)
    qkv = matmul(
        h, [w_qkv],
        functools.partial(_epi_qkv, n_q_blocks=cfg.n_q_heads * hd // tn_qkv,
                          n_k_blocks=cfg.n_kv_heads * hd // tn_qkv, head_dim=hd, eps=cfg.eps),
        out_cols=w_qkv.shape[1], out_dtype=BF16, tm=TM, tn=tn_qkv, tk=d,
        extras=[tab(cos), tab(sin_lo), tab(sin_hi),
                vec(p["attn_q_norm"][0] * q_scale), vec(p["attn_k_norm"][0])],
        name="qkv_proj")
    score_bound = (hd * jnp.max(jnp.abs(p["attn_q_norm"][0] * q_scale)) * jnp.max(jnp.abs(p["attn_k_norm"][0]))
                   * (1.0 + 2.0 ** -6))
    o = attention(qkv, score_bound, cfg)
    res = lambda arr, tn: (arr, (TM, tn), lambda i, j: (i, j))
    x = matmul(o, [bf(p["attn_wo"][0])], functools.partial(_epi_residual_pair, a_blocks=a_blocks),
               out_cols=d, out_dtype=F32, tm=TM, tn=512, tk=o.shape[1],
               extras=[(xa, (TM, 512), lambda i, j: (jnp.minimum(i, a_blocks - 1),
                                                     jnp.where(i < a_blocks, j, d // 512 - 1))),
                       (xb, (TM, 512), lambda i, j: (jnp.maximum(i - a_blocks, 0),
                                                     jnp.where(i < a_blocks, 0, j)))],
               name="attn_out")

    h = rmsnorm(x, p["norm_ffn"][0], BF16, cfg)
    hid = matmul(h, [bf(p["ffn_w1"][0]), bf(p["ffn_w3"][0])], _epi_swiglu,
                 out_cols=cfg.d_ff, out_dtype=BF16, tm=TM, tn=512, tk=d, name="ffn_up")
    x = matmul(hid, [bf(p["ffn_w2"][0])], _epi_residual, out_cols=d, out_dtype=F32,
               tm=TM, tn=1024, tk=2048, extras=[res(x, 1024)], name="ffn_down")

    nh, dk, dv, rank = cfg.gla_heads, cfg.gla_dk, cfg.gla_dv, cfg.gla_rank
    h = rmsnorm(x, p["norm_mix"][1], BF16, cfg)
    qk = matmul(h, [bf(jnp.concatenate([p["gla_wq"][0], p["gla_wk"][0]], axis=1))], _epi_store,
                out_cols=2 * nh * dk, out_dtype=F32, tm=TM, tn=1024, tk=d, name="gla_qk")
    v = matmul(h, [bf(p["gla_wv"][0])], _epi_store, out_cols=nh * dv, out_dtype=BF16,
               tm=TM, tn=1024, tk=d, name="gla_v")
    wa1 = jnp.zeros((d, V7X_LANES), F32)
    wa1 = wa1.at[:, :rank].set(p["gla_wa1_f"][0]).at[:, rank:2 * rank].set(p["gla_wa1_b"][0])
    a = matmul(h, [bf(wa1)], _epi_store, out_cols=V7X_LANES, out_dtype=F32,
               tm=TM, tn=V7X_LANES, tk=d, name="gla_gate_lowrank")
    wa2 = jnp.zeros((2, V7X_LANES, nh * dk), F32)
    wa2 = wa2.at[0, :rank].set(p["gla_wa2_f"][0]).at[1, rank:2 * rank].set(p["gla_wa2_b"][0])
    ba = jnp.stack([p["gla_ba_f"][0], p["gla_ba_b"][0]]).reshape(2, 1, nh * dk)
    o_f, o_b = gla_scan(qk, v, a, bf(wa2), ba, cfg)
    gated = matmul(
        h, [bf(p["gla_wg"][0])], functools.partial(_epi_gla_gate, eps=cfg.eps),
        out_cols=nh * dv, out_dtype=BF16, tm=TM // 2, tn=dv, tk=d,
        extras=[(o_f, (TM // 2, dv), lambda i, j: (i, j)), (o_b, (TM // 2, dv), lambda i, j: (i, j)),
                (jnp.tile(p["gla_o_norm"][0], nh).reshape(1, nh * dv), (1, dv), lambda i, j: (0, j))],
        name="gla_gate")
    x = matmul(gated, [bf(p["gla_wo"][0])], _epi_residual, out_cols=d, out_dtype=F32,
               tm=TM, tn=1024, tk=nh * dv, extras=[res(x, 1024)], name="gla_out")

    route = router(x, p["norm_ffn"][1], p["moe_router"][0], cfg)
    return routed_moe_final(x, route, p["norm_ffn"][1], bf(p["moe_w1"][0]), bf(p["moe_w3"][0]),
                            bf(p["moe_w2"][0]), p["norm_final"], na, cfg)


def kernel(x_prompt, x_sample, norm_mix, norm_ffn, norm_final, attn_wq, attn_wk, attn_wv, attn_q_norm, attn_k_norm, attn_wo, gla_wq, gla_wk, gla_wv, gla_wg, gla_wa1_f, gla_wa2_f, gla_ba_f, gla_wa1_b, gla_wa2_b, gla_ba_b, gla_o_norm, gla_wo, ffn_w1, ffn_w3, ffn_w2, moe_router, moe_w1, moe_w3, moe_w2):
    cfg = PROD
    d = cfg.d_model
    params = dict(
        norm_mix=norm_mix, norm_ffn=norm_ffn, norm_final=norm_final,
        attn_wq=attn_wq, attn_wk=attn_wk, attn_wv=attn_wv, attn_q_norm=attn_q_norm,
        attn_k_norm=attn_k_norm, attn_wo=attn_wo,
        gla_wq=gla_wq, gla_wk=gla_wk, gla_wv=gla_wv, gla_wg=gla_wg,
        gla_wa1_f=gla_wa1_f, gla_wa2_f=gla_wa2_f, gla_ba_f=gla_ba_f,
        gla_wa1_b=gla_wa1_b, gla_wa2_b=gla_wa2_b, gla_ba_b=gla_ba_b,
        gla_o_norm=gla_o_norm, gla_wo=gla_wo,
        ffn_w1=ffn_w1, ffn_w3=ffn_w3, ffn_w2=ffn_w2,
        moe_router=moe_router, moe_w1=moe_w1, moe_w3=moe_w3, moe_w2=moe_w2)
    y_prompt, y_sample = _trunk(x_prompt.reshape(-1, d), x_sample.reshape(-1, d), params, cfg)
    return (y_prompt.reshape(x_prompt.shape), y_sample.reshape(x_sample.shape))
```
